```python
import jax, jax.numpy as jnp
from jax import lax
import numpy as np

D_MODEL = 1024
BATCH = 8
SEQ = 2048
DEPTH = 1

D_CONV = D_MODEL // 2
CONV_WIDTH = 3
N_HEADS = 8
N_KV_HEADS = 2
HEAD_DIM = 64
D_ATTN = N_HEADS * HEAD_DIM
IDX_HEADS = 8
IDX_DIM = 64
TOPK_MAX = 256
Q_BLOCK = 128
N_BRANCH = 2
D_FF = 4 * D_MODEL
LN_EPS = 1e-5
ALPHA = (2 * DEPTH) ** 0.25
BETA = (8 * DEPTH) ** -0.25

IN_SIZES = (D_CONV, D_CONV, D_CONV,
            N_HEADS * HEAD_DIM, N_KV_HEADS * HEAD_DIM, N_KV_HEADS * HEAD_DIM,
            IDX_HEADS * IDX_DIM, IDX_DIM, IDX_HEADS,
            N_BRANCH * D_MODEL)
D_IN = sum(IN_SIZES)

kernel_name = "hybrid_gatedconv_dsa_sqrelu_deepnorm"


def layer_norm(x, g, b):
    xf = x.astype(jnp.float32)
    mu = jnp.mean(xf, axis=-1, keepdims=True)
    xc = xf - mu
    var = jnp.mean(xc * xc, axis=-1, keepdims=True)
    y = xc * lax.rsqrt(var + LN_EPS) * g.astype(jnp.float32) + b.astype(jnp.float32)
    return y.astype(x.dtype)


def causal_depthwise_conv(v, w):
    c = v.shape[-1]
    rhs = w[:, None, :]
    return lax.conv_general_dilated(
        v, rhs, window_strides=(1,), padding=((CONV_WIDTH - 1, 0),),
        dimension_numbers=("NWC", "WIO", "NWC"), feature_group_count=c)


def dsa_sparse_attention(q, k, v, qi, ki, wi):
    b, s = q.shape[0], q.shape[1]
    k_sel = min(TOPK_MAX, s // 4)
    nb = s // Q_BLOCK
    idx_scale = (IDX_DIM ** -0.5) * (IDX_HEADS ** -0.5)
    attn_scale = HEAD_DIM ** -0.5
    rep = N_HEADS // N_KV_HEADS
    key_pos = jnp.arange(s)

    def to_blocks(a):
        a = a.reshape((b, nb, Q_BLOCK) + a.shape[2:])
        return jnp.moveaxis(a, 1, 0)

    def one_block(args):
        qb, qib, wib, blk = args
        t = blk * Q_BLOCK + jnp.arange(Q_BLOCK)
        logits = jnp.einsum("bqhd,bsd->bqhs", qib, ki).astype(jnp.float32)
        score = jnp.einsum("bqh,bqhs->bqs", wib.astype(jnp.float32),
                           jax.nn.relu(logits)) * idx_scale
        causal = key_pos[None, :] <= t[:, None]
        score = jnp.where(causal[None], score, -jnp.inf)
        _, sel = lax.top_k(score, k_sel)
        valid = sel <= t[None, :, None]
        kg = jax.vmap(lambda kk, ii: kk[ii])(k, sel)
        vg = jax.vmap(lambda vv, ii: vv[ii])(v, sel)
        qg = qb.reshape(b, Q_BLOCK, N_KV_HEADS, rep, HEAD_DIM)
        att = jnp.einsum("bqgrd,bqkgd->bqgrk", qg, kg).astype(jnp.float32) * attn_scale
        att = jnp.where(valid[:, :, None, None, :], att, -jnp.inf)
        p = jax.nn.softmax(att, axis=-1).astype(vg.dtype)
        o = jnp.einsum("bqgrk,bqkgd->bqgrd", p, vg)
        return o.reshape(b, Q_BLOCK, D_ATTN)

    out = lax.map(one_block, (to_blocks(q), to_blocks(qi), to_blocks(wi), jnp.arange(nb)))
    return jnp.moveaxis(out, 0, 1).reshape(b, s, D_ATTN)


def setup_inputs(seed: int = 0) -> dict:
    key = jax.random.key(seed)
    ks = jax.random.split(key, 16)
    f32 = jnp.float32
    nrm = lambda k, shape, scale: jax.random.normal(k, shape, f32) * scale
    x = jax.random.normal(ks[0], (BATCH, SEQ, D_MODEL), f32)
    w_in = nrm(ks[1], (DEPTH, D_MODEL, D_IN), D_MODEL ** -0.5)
    conv_w = nrm(ks[2], (DEPTH, CONV_WIDTH, D_CONV), CONV_WIDTH ** -0.5)
    idx_k_norm_g = 1.0 + nrm(ks[3], (DEPTH, IDX_DIM), 0.02)
    idx_k_norm_b = nrm(ks[4], (DEPTH, IDX_DIM), 0.02)
    w_branch = nrm(ks[5], (DEPTH, N_BRANCH, D_CONV, D_MODEL), BETA * D_CONV ** -0.5)
    w_o = nrm(ks[6], (DEPTH, D_MODEL, D_MODEL), BETA * D_MODEL ** -0.5)
    ln1_g = 1.0 + nrm(ks[7], (DEPTH, D_MODEL), 0.02)
    ln1_b = nrm(ks[8], (DEPTH, D_MODEL), 0.02)
    w_up = nrm(ks[9], (DEPTH, D_MODEL, D_FF), BETA * D_MODEL ** -0.5)
    w_down = nrm(ks[10], (DEPTH, D_FF, D_MODEL), BETA * D_FF ** -0.5)
    ln2_g = 1.0 + nrm(ks[11], (DEPTH, D_MODEL), 0.02)
    ln2_b = nrm(ks[12], (DEPTH, D_MODEL), 0.02)
    return {"x": x, "w_in": w_in, "conv_w": conv_w,
            "idx_k_norm_g": idx_k_norm_g, "idx_k_norm_b": idx_k_norm_b,
            "w_branch": w_branch, "w_o": w_o, "ln1_g": ln1_g, "ln1_b": ln1_b,
            "w_up": w_up, "w_down": w_down, "ln2_g": ln2_g, "ln2_b": ln2_b}


def reference(x, w_in, conv_w, idx_k_norm_g, idx_k_norm_b, w_branch, w_o,
              ln1_g, ln1_b, w_up, w_down, ln2_g, ln2_b):
    b, s, _ = x.shape
    offs = np.cumsum(IN_SIZES)[:-1].tolist()
    for l in range(DEPTH):
        z = x @ w_in[l]
        (b_gate, c_gate, u, q, k, v, qi, ki, wi, gates) = jnp.split(z, offs, axis=-1)
        y_a = b_gate * causal_depthwise_conv(c_gate * u, conv_w[l])
        ki = layer_norm(ki, idx_k_norm_g[l], idx_k_norm_b[l])
        y_b = dsa_sparse_attention(
            q.reshape(b, s, N_HEADS, HEAD_DIM),
            k.reshape(b, s, N_KV_HEADS, HEAD_DIM),
            v.reshape(b, s, N_KV_HEADS, HEAD_DIM),
            qi.reshape(b, s, IDX_HEADS, IDX_DIM), ki, wi)
        y_br = jnp.stack([y_a, y_b], axis=2)
        proj = jnp.einsum("bsnc,ncd->bsnd", y_br, w_branch[l])
        g = jax.nn.sigmoid(gates.reshape(b, s, N_BRANCH, D_MODEL))
        mix = jnp.sum(g * proj, axis=2) @ w_o[l]
        h = layer_norm(ALPHA * x + mix, ln1_g[l], ln1_b[l])
        ff = jnp.square(jax.nn.relu(h @ w_up[l])) @ w_down[l]
        x = layer_norm(ALPHA * h + ff, ln2_g[l], ln2_b[l])
    return x
```

```python
import functools

import jax
import jax.numpy as jnp
from jax import lax
from jax.experimental import pallas as pl
from jax.experimental.pallas import tpu as pltpu

D_MODEL = 1024
D_CONV = 512
CONV_WIDTH = 3
N_HEADS = 8
N_KV_HEADS = 2
HEAD_DIM = 64
D_ATTN = N_HEADS * HEAD_DIM
D_KV = N_KV_HEADS * HEAD_DIM
IDX_HEADS = 8
IDX_DIM = 64
TOPK_MAX = 256
N_BRANCH = 2
D_FF = 4 * D_MODEL
LN_EPS = 1e-5
IDX_SCALE = (IDX_DIM ** -0.5) * (IDX_HEADS ** -0.5)
ATTN_SCALE = HEAD_DIM ** -0.5

OFF_Q = 3 * D_CONV
OFF_K = OFF_Q + D_ATTN
OFF_V = OFF_K + D_KV
OFF_QI = OFF_V + D_KV
OFF_KI = OFF_QI + IDX_HEADS * IDX_DIM
OFF_WI = OFF_KI + IDX_DIM
OFF_GATES = OFF_WI + IDX_HEADS

LANES = 128
SUBLANES = 8
VMEM_LIMIT = 56 * 1024 * 1024

TM_PROJ = 512
TQ = 256
TK = 256
TM_TAIL = 512
FF_CHUNK = 1024

F32 = jnp.float32
BF16 = jnp.bfloat16
NEG_INF = float("-inf")
F32_MAX = float(jnp.finfo(jnp.float32).max)
INT_MIN = -2 ** 31


def _rep(x, width):
    return jnp.concatenate([x] * (width // LANES), axis=1)


def _layer_norm_rows(x, g, b):
    mu = jnp.mean(x, axis=-1, keepdims=True)
    xc = x - mu
    var = jnp.mean(xc * xc, axis=-1, keepdims=True)
    return xc * lax.rsqrt(var + LN_EPS) * g + b


def _proj_kernel(x_ref, wa_ref, wkw_ref, convw_ref, kg_ref, kb_ref,
                 ya_ref, q_ref, kt_ref, v_ref, qi_ref, kit_ref, wi_ref,
                 carry_ref):
    j = pl.program_id(1)
    xb = x_ref[0].astype(BF16)

    def proj(lo, hi):
        return jnp.dot(xb, wa_ref[:, lo:hi], preferred_element_type=F32)

    cu = proj(D_CONV, 2 * D_CONV) * proj(2 * D_CONV, 3 * D_CONV)

    @pl.when(j == 0)
    def _():
        carry_ref[...] = jnp.zeros_like(carry_ref)

    prev = carry_ref[...]
    row = lax.broadcasted_iota(jnp.int32, cu.shape, 0)
    cu1 = jnp.where(row == 0, prev[SUBLANES - 1:SUBLANES, :], pltpu.roll(cu, 1, axis=0))
    cu2 = jnp.where(row == 0, prev[SUBLANES - 2:SUBLANES - 1, :],
                    jnp.where(row == 1, prev[SUBLANES - 1:SUBLANES, :], pltpu.roll(cu, 2, axis=0)))
    carry_ref[...] = cu[TM_PROJ - SUBLANES:, :]
    cw = convw_ref[...]
    conv = cw[0:1, :] * cu2 + cw[1:2, :] * cu1 + cw[2:3, :] * cu
    ya_ref[0] = (proj(0, D_CONV) * conv).astype(BF16)

    q_ref[0] = (proj(OFF_Q, OFF_K) * ATTN_SCALE).astype(BF16)
    kt = proj(OFF_K, OFF_V).T
    for c in range(TM_PROJ // TK):
        kt_ref[0, c] = kt[:, c * TK:(c + 1) * TK].astype(BF16)
    v_ref[0] = proj(OFF_V, OFF_QI).astype(BF16)
    qi_ref[0] = proj(OFF_QI, OFF_KI).astype(BF16)

    zkw = jnp.dot(xb, wkw_ref[...], preferred_element_type=F32)
    wi_ref[0] = zkw[:, LANES:]
    zk = zkw[:, :LANES]
    valid = lax.broadcasted_iota(jnp.int32, zk.shape, 1) < IDX_DIM
    mu = jnp.sum(jnp.where(valid, zk, 0.0), axis=-1, keepdims=True) * (1.0 / IDX_DIM)
    xc = jnp.where(valid, zk - mu, 0.0)
    var = jnp.sum(xc * xc, axis=-1, keepdims=True) * (1.0 / IDX_DIM)
    kin = xc * lax.rsqrt(var + LN_EPS) * kg_ref[...] + kb_ref[...]
    kit = kin.T[:IDX_DIM, :]
    for c in range(TM_PROJ // TK):
        kit_ref[0, c] = kit[:, c * TK:(c + 1) * TK].astype(BF16)


def _run_proj(x, wa, wkw, conv_w, kg, kb):
    b, s, d = x.shape
    nc = s // TK
    cpt = TM_PROJ // TK
    const = lambda *shape: pl.BlockSpec(shape, lambda bi, j: (0,) * len(shape),
                                        pipeline_mode=pl.Buffered(1))
    rows = lambda width: pl.BlockSpec((1, TM_PROJ, width), lambda bi, j: (bi, j, 0))
    return pl.pallas_call(
        _proj_kernel,
        grid=(b, s // TM_PROJ),
        in_specs=[rows(d), const(d, OFF_KI), const(d, 2 * LANES), const(CONV_WIDTH, D_CONV),
                  const(1, LANES), const(1, LANES)],
        out_specs=[rows(D_CONV), rows(D_ATTN),
                   pl.BlockSpec((1, cpt, D_KV, TK), lambda bi, j: (bi, j, 0, 0)),
                   rows(D_KV), rows(IDX_HEADS * IDX_DIM),
                   pl.BlockSpec((1, cpt, IDX_DIM, TK), lambda bi, j: (bi, j, 0, 0)),
                   rows(LANES)],
        out_shape=[jax.ShapeDtypeStruct((b, s, D_CONV), BF16),
                   jax.ShapeDtypeStruct((b, s, D_ATTN), BF16),
                   jax.ShapeDtypeStruct((b, nc, D_KV, TK), BF16),
                   jax.ShapeDtypeStruct((b, s, D_KV), BF16),
                   jax.ShapeDtypeStruct((b, s, IDX_HEADS * IDX_DIM), BF16),
                   jax.ShapeDtypeStruct((b, nc, IDX_DIM, TK), BF16),
                   jax.ShapeDtypeStruct((b, s, LANES), F32)],
        scratch_shapes=[pltpu.VMEM((SUBLANES, D_CONV), F32)],
        compiler_params=pltpu.CompilerParams(
            dimension_semantics=("parallel", "arbitrary"), vmem_limit_bytes=VMEM_LIMIT),
        name="proj_mixer_a",
    )(x, wa, wkw, conv_w, kg, kb)


def _sortable_to_f32(x):
    bits = x ^ ((x >> 31) & jnp.int32(0x7FFFFFFF))
    return pltpu.bitcast(bits, F32)


def _attn_kernel(qi_ref, wi_ref, kit_ref, q_ref, kt_ref, v_ref, yb_ref,
                 score_ref, bias_ref, wb_ref, m_ref, l_ref, acc_ref, *, k_sel):
    i = pl.program_id(1)
    nk = i + 1
    kf = float(k_sel)

    w = wi_ref[0]
    for h in range(IDX_HEADS):
        wb_ref[h] = jnp.broadcast_to(w[:, h:h + 1], (TQ, LANES))
    row = lax.broadcasted_iota(jnp.int32, (TQ, TK), 0)
    col = lax.broadcasted_iota(jnp.int32, (TQ, TK), 1)
    col_minus_row = col - row

    def score_body(c, carry):
        kic = kit_ref[0, c]
        acc = jnp.zeros((TQ, TK), F32)
        for h in range(IDX_HEADS):
            lg = jnp.dot(qi_ref[0, :, h * IDX_DIM:(h + 1) * IDX_DIM], kic,
                         preferred_element_type=F32)
            acc = acc + _rep(wb_ref[h], TK) * jnp.maximum(lg, 0.0)
        noncausal = col_minus_row > (i - c) * TQ
        score_ref[c] = jnp.where(noncausal, NEG_INF, acc * IDX_SCALE)
        return carry

    lax.fori_loop(0, nk, score_body, 0)

    def count_where(pred):
        def body(c, acc):
            cnt = jnp.where(pred(score_ref[c]), 1.0, 0.0)
            for g in range(TK // LANES):
                acc = acc + cnt[:, g * LANES:(g + 1) * LANES]
            return acc
        acc = lax.fori_loop(0, nk, body, jnp.zeros((TQ, LANES), F32))
        return jnp.broadcast_to(jnp.sum(acc, axis=1, keepdims=True), (TQ, LANES))

    def count_ge(cand_key):
        cand = _rep(_sortable_to_f32(cand_key), TK)
        return count_where(lambda sc: sc >= cand)

    cnt0 = count_ge(jnp.zeros((TQ, LANES), jnp.int32))
    ok0 = cnt0 >= kf
    prefix0 = jnp.where(ok0, jnp.int32(0), jnp.int32(INT_MIN))
    cntp0 = jnp.where(ok0, cnt0, 0.0)

    def bit_body(b, carry):
        prefix, cntp = carry
        cand_key = prefix | jnp.left_shift(jnp.int32(1), 30 - b)
        cnt = count_ge(cand_key)
        ok = cnt >= kf
        return jnp.where(ok, cand_key, prefix), jnp.where(ok, cnt, cntp)

    prefix, cntp = lax.fori_loop(0, 31, bit_body, (prefix0, cntp0))
    tau = jnp.where(prefix == INT_MIN, -F32_MAX, _sortable_to_f32(prefix))
    tau_w = _rep(tau, TK)

    def bias_body(c, carry):
        bias_ref[c] = jnp.where(score_ref[c] >= tau_w, 0.0, NEG_INF)
        return carry

    lax.fori_loop(0, nk, bias_body, 0)

    @pl.when(jnp.max(cntp) > kf)
    def _():
        need = _rep(kf - count_where(lambda sc: sc > tau_w), TK)
        r = lax.broadcasted_iota(jnp.int32, (TK, TK), 0)
        cc = lax.broadcasted_iota(jnp.int32, (TK, TK), 1)
        before = jnp.where(r < cc, 1.0, 0.0).astype(BF16)
        ones = jnp.ones((TK, LANES), BF16)

        def tie_body(c, seen):
            sc = score_ref[c]
            eq = sc == tau_w
            eqb = jnp.where(eq, 1.0, 0.0).astype(BF16)
            rank = jnp.dot(eqb, before, preferred_element_type=F32) + _rep(seen, TK)
            keep = (sc > tau_w) | (eq & (rank < need))
            bias_ref[c] = jnp.where(keep, 0.0, NEG_INF)
            return seen + jnp.dot(eqb, ones, preferred_element_type=F32)

        lax.fori_loop(0, nk, tie_body, jnp.zeros((TQ, LANES), F32))

    m_ref[...] = jnp.full(m_ref.shape, NEG_INF, F32)
    l_ref[...] = jnp.zeros(l_ref.shape, F32)
    acc_ref[...] = jnp.zeros(acc_ref.shape, F32)
    rep_heads = N_HEADS // N_KV_HEADS

    def att_body(c, carry):
        bias = bias_ref[c]
        vc = v_ref[0, c]
        for h in range(N_HEADS):
            g = h // rep_heads
            s = jnp.dot(q_ref[0, :, h * HEAD_DIM:(h + 1) * HEAD_DIM],
                        kt_ref[0, c, g * HEAD_DIM:(g + 1) * HEAD_DIM, :],
                        preferred_element_type=F32) + bias
            m_old = m_ref[h]
            m_new = jnp.maximum(m_old, jnp.max(s, axis=1, keepdims=True))
            m_use = jnp.where(m_new == NEG_INF, 0.0, m_new)
            alpha = jnp.exp(m_old - m_use)
            p = jnp.exp(s - _rep(m_use, TK))
            l_ref[h] = alpha * l_ref[h] + jnp.sum(p, axis=1, keepdims=True)
            acc_ref[h] = alpha * acc_ref[h] + jnp.dot(p.astype(BF16), vc, preferred_element_type=F32)
            m_ref[h] = m_new
        return carry

    lax.fori_loop(0, nk, att_body, 0)

    low = lax.broadcasted_iota(jnp.int32, (TQ, LANES), 1) < HEAD_DIM
    for pair in range(N_HEADS // 2):
        h0, h1 = 2 * pair, 2 * pair + 1
        o0 = acc_ref[h0] / l_ref[h0]
        o1 = acc_ref[h1] / l_ref[h1]
        if h0 // rep_heads == 0:
            o = jnp.where(low, o0, pltpu.roll(o1, HEAD_DIM, axis=1))
        else:
            o = jnp.where(low, pltpu.roll(o0, HEAD_DIM, axis=1), o1)
        yb_ref[0, :, pair * LANES:(pair + 1) * LANES] = o.astype(BF16)


def _run_attn(qi, wi, kit, q, kt, v, k_sel):
    b, s, _ = q.shape
    nc = s // TK
    v = v.reshape(b, nc, TK, D_KV)
    rows = lambda width: pl.BlockSpec((1, TQ, width), lambda bi, i: (bi, i, 0))
    keys = lambda d0, d1: pl.BlockSpec((1, nc, d0, d1), lambda bi, i: (bi, 0, 0, 0))
    return pl.pallas_call(
        functools.partial(_attn_kernel, k_sel=k_sel),
        grid=(b, s // TQ),
        in_specs=[rows(IDX_HEADS * IDX_DIM), rows(LANES), keys(IDX_DIM, TK),
                  rows(D_ATTN), keys(D_KV, TK), keys(TK, D_KV)],
        out_specs=rows(D_ATTN),
        out_shape=jax.ShapeDtypeStruct((b, s, D_ATTN), BF16),
        scratch_shapes=[pltpu.VMEM((nc, TQ, TK), F32),
                        pltpu.VMEM((nc, TQ, TK), F32),
                        pltpu.VMEM((IDX_HEADS, TQ, LANES), F32),
                        pltpu.VMEM((N_HEADS, TQ, LANES), F32),
                        pltpu.VMEM((N_HEADS, TQ, LANES), F32),
                        pltpu.VMEM((N_HEADS, TQ, D_KV), F32)],
        compiler_params=pltpu.CompilerParams(
            dimension_semantics=("parallel", "parallel"), vmem_limit_bytes=VMEM_LIMIT),
        name="dsa_attention",
    )(qi, wi, kit, q, kt, v)


def _tail_kernel(x_ref, ya_ref, yb_ref, wg_ref, wbr_ref, wo_ref, wup_ref, wdn_ref,
                 g1_ref, b1_ref, g2_ref, b2_ref, out_ref, *, alpha):
    x = x_ref[...]
    xb = x.astype(BF16)
    merged = jnp.zeros((TM_TAIL, D_MODEL), F32)
    for n, y_ref in enumerate((ya_ref, yb_ref)):
        gate = jnp.dot(xb, wg_ref[:, n * D_MODEL:(n + 1) * D_MODEL], preferred_element_type=F32)
        branch = jnp.dot(y_ref[...], wbr_ref[n], preferred_element_type=F32)
        merged = merged + jax.nn.sigmoid(gate) * branch
    mix = jnp.dot(merged.astype(BF16), wo_ref[...], preferred_element_type=F32)
    h = _layer_norm_rows(alpha * x + mix, g1_ref[...], b1_ref[...])
    hb = h.astype(BF16)
    ff = jnp.zeros((TM_TAIL, D_MODEL), F32)
    for f in range(D_FF // FF_CHUNK):
        up = jnp.dot(hb, wup_ref[:, f * FF_CHUNK:(f + 1) * FF_CHUNK], preferred_element_type=F32)
        act = jnp.square(jnp.maximum(up, 0.0)).astype(BF16)
        ff = ff + jnp.dot(act, wdn_ref[f * FF_CHUNK:(f + 1) * FF_CHUNK, :], preferred_element_type=F32)
    out_ref[...] = _layer_norm_rows(alpha * h + ff, g2_ref[...], b2_ref[...])


def _run_tail(x2, ya2, yb2, wg, wbr, wo, wup, wdn, g1, b1, g2, b2, alpha):
    n, d = x2.shape
    const = lambda *shape: pl.BlockSpec(shape, lambda r: (0,) * len(shape),
                                        pipeline_mode=pl.Buffered(1))
    rows = lambda width: pl.BlockSpec((TM_TAIL, width), lambda r: (r, 0))
    return pl.pallas_call(
        functools.partial(_tail_kernel, alpha=alpha),
        grid=(n // TM_TAIL,),
        in_specs=[rows(d), rows(D_CONV), rows(D_ATTN),
                  const(d, N_BRANCH * d), const(N_BRANCH, D_CONV, d), const(d, d),
                  const(d, D_FF), const(D_FF, d),
                  const(1, d), const(1, d), const(1, d), const(1, d)],
        out_specs=rows(d),
        out_shape=jax.ShapeDtypeStruct((n, d), F32),
        compiler_params=pltpu.CompilerParams(
            dimension_semantics=("parallel",), vmem_limit_bytes=VMEM_LIMIT),
        name="merge_mlp",
    )(x2, ya2, yb2, wg, wbr, wo, wup, wdn, g1, b1, g2, b2)


def kernel(x, w_in, conv_w, idx_k_norm_g, idx_k_norm_b, w_branch, w_o, ln1_g, ln1_b, w_up, w_down, ln2_g, ln2_b):
    b, s, d = x.shape
    depth = w_in.shape[0]
    assert d == D_MODEL and s % TM_PROJ == 0 and (b * s) % TM_TAIL == 0 and TQ == TK
    k_sel = min(TOPK_MAX, s // 4)
    alpha = (2 * depth) ** 0.25
    for l in range(depth):
        wl = w_in[l]
        wa = wl[:, :OFF_KI].astype(BF16)
        wkw = jnp.concatenate(
            [wl[:, OFF_KI:OFF_WI], jnp.zeros((d, LANES - IDX_DIM), F32),
             wl[:, OFF_WI:OFF_GATES], jnp.zeros((d, LANES - IDX_HEADS), F32)], axis=1).astype(BF16)
        wg = wl[:, OFF_GATES:].astype(BF16)
        pad = lambda a: jnp.pad(a.astype(F32), (0, LANES - IDX_DIM)).reshape(1, LANES)
        ya, q, kt, v, qi, kit, wi = _run_proj(x, wa, wkw, conv_w[l], pad(idx_k_norm_g[l]), pad(idx_k_norm_b[l]))
        yb = _run_attn(qi, wi, kit, q, kt, v, k_sel)
        row = lambda a: a.astype(F32).reshape(1, d)
        out = _run_tail(x.reshape(b * s, d), ya.reshape(b * s, D_CONV), yb.reshape(b * s, D_ATTN),
                        wg, w_branch[l].astype(BF16), w_o[l].astype(BF16),
                        w_up[l].astype(BF16), w_down[l].astype(BF16),
                        row(ln1_g[l]), row(ln1_b[l]), row(ln2_g[l]), row(ln2_b[l]), alpha)
        x = out.reshape(b, s, d)
    return x
```

```python
import functools

import jax
import jax.numpy as jnp
from jax import lax
from jax.experimental import pallas as pl
from jax.experimental.pallas import tpu as pltpu

D_MODEL = 1024
D_CONV = 512
CONV_WIDTH = 3
N_HEADS = 8
N_KV_HEADS = 2
HEAD_DIM = 64
D_ATTN = N_HEADS * HEAD_DIM
D_KV = N_KV_HEADS * HEAD_DIM
IDX_HEADS = 8
IDX_DIM = 64
TOPK_MAX = 256
N_BRANCH = 2
D_FF = 4 * D_MODEL
LN_EPS = 1e-5
IDX_SCALE = (IDX_DIM ** -0.5) * (IDX_HEADS ** -0.5)
ATTN_SCALE = HEAD_DIM ** -0.5

OFF_Q = 3 * D_CONV
OFF_K = OFF_Q + D_ATTN
OFF_V = OFF_K + D_KV
OFF_QI = OFF_V + D_KV
OFF_KI = OFF_QI + IDX_HEADS * IDX_DIM
OFF_WI = OFF_KI + IDX_DIM
OFF_GATES = OFF_WI + IDX_HEADS

LANES = 128
SUBLANES = 8
VMEM_LIMIT = 56 * 1024 * 1024

TM_PROJ = 512
TQ = 256
TK = 256
TM_TAIL = 512
FF_CHUNK = 1024

F32 = jnp.float32
BF16 = jnp.bfloat16
NEG_INF = float("-inf")
F32_MAX = float(jnp.finfo(jnp.float32).max)
INT_MIN = -2 ** 31


def _layer_norm_rows(x, g, b):
    mu = jnp.mean(x, axis=-1, keepdims=True)
    xc = x - mu
    var = jnp.mean(xc * xc, axis=-1, keepdims=True)
    return xc * lax.rsqrt(var + LN_EPS) * g + b


def _proj_kernel(x_ref, wa_ref, wkw_ref, convw_ref, kg_ref, kb_ref,
                 ya_ref, q_ref, k_ref, vt_ref, qi_ref, ki_ref, wi_ref,
                 carry_ref):
    j = pl.program_id(1)
    xb = x_ref[0].astype(BF16)

    def proj(lo, hi):
        return jnp.dot(xb, wa_ref[:, lo:hi], preferred_element_type=F32)

    cu = proj(D_CONV, 2 * D_CONV) * proj(2 * D_CONV, 3 * D_CONV)

    @pl.when(j == 0)
    def _():
        carry_ref[...] = jnp.zeros_like(carry_ref)

    prev = carry_ref[...]
    row = lax.broadcasted_iota(jnp.int32, cu.shape, 0)
    cu1 = jnp.where(row == 0, prev[SUBLANES - 1:SUBLANES, :], pltpu.roll(cu, 1, axis=0))
    cu2 = jnp.where(row == 0, prev[SUBLANES - 2:SUBLANES - 1, :],
                    jnp.where(row == 1, prev[SUBLANES - 1:SUBLANES, :], pltpu.roll(cu, 2, axis=0)))
    carry_ref[...] = cu[TM_PROJ - SUBLANES:, :]
    cw = convw_ref[...]
    conv = cw[0:1, :] * cu2 + cw[1:2, :] * cu1 + cw[2:3, :] * cu
    ya_ref[0] = (proj(0, D_CONV) * conv).astype(BF16)

    q_ref[0] = (proj(OFF_Q, OFF_K) * ATTN_SCALE).astype(BF16)
    k_ref[0] = proj(OFF_K, OFF_V).astype(BF16)
    vt = proj(OFF_V, OFF_QI).T
    for c in range(TM_PROJ // TK):
        vt_ref[0, c] = vt[:, c * TK:(c + 1) * TK].astype(BF16)
    qi_ref[0] = proj(OFF_QI, OFF_KI).astype(BF16)

    zkw = jnp.dot(xb, wkw_ref[...], preferred_element_type=F32)
    wi_ref[0] = zkw[:, LANES:]
    zk = zkw[:, :LANES]
    valid = lax.broadcasted_iota(jnp.int32, zk.shape, 1) < IDX_DIM
    mu = jnp.sum(jnp.where(valid, zk, 0.0), axis=-1, keepdims=True) * (1.0 / IDX_DIM)
    xc = jnp.where(valid, zk - mu, 0.0)
    var = jnp.sum(xc * xc, axis=-1, keepdims=True) * (1.0 / IDX_DIM)
    kin = xc * lax.rsqrt(var + LN_EPS) * kg_ref[...] + kb_ref[...]
    ki_ref[0] = kin[:, :IDX_DIM].astype(BF16)


def _run_proj(x, wa, wkw, conv_w, kg, kb):
    b, s, d = x.shape
    nc = s // TK
    cpt = TM_PROJ // TK
    const = lambda *shape: pl.BlockSpec(shape, lambda bi, j: (0,) * len(shape),
                                        pipeline_mode=pl.Buffered(1))
    rows = lambda width: pl.BlockSpec((1, TM_PROJ, width), lambda bi, j: (bi, j, 0))
    return pl.pallas_call(
        _proj_kernel,
        grid=(b, s // TM_PROJ),
        in_specs=[rows(d), const(d, OFF_KI), const(d, 2 * LANES), const(CONV_WIDTH, D_CONV),
                  const(1, LANES), const(1, LANES)],
        out_specs=[rows(D_CONV), rows(D_ATTN), rows(D_KV),
                   pl.BlockSpec((1, cpt, D_KV, TK), lambda bi, j: (bi, j, 0, 0)),
                   rows(IDX_HEADS * IDX_DIM), rows(IDX_DIM), rows(LANES)],
        out_shape=[jax.ShapeDtypeStruct((b, s, D_CONV), BF16),
                   jax.ShapeDtypeStruct((b, s, D_ATTN), BF16),
                   jax.ShapeDtypeStruct((b, s, D_KV), BF16),
                   jax.ShapeDtypeStruct((b, nc, D_KV, TK), BF16),
                   jax.ShapeDtypeStruct((b, s, IDX_HEADS * IDX_DIM), BF16),
                   jax.ShapeDtypeStruct((b, s, IDX_DIM), BF16),
                   jax.ShapeDtypeStruct((b, s, LANES), F32)],
        scratch_shapes=[pltpu.VMEM((SUBLANES, D_CONV), F32)],
        compiler_params=pltpu.CompilerParams(
            dimension_semantics=("parallel", "arbitrary"), vmem_limit_bytes=VMEM_LIMIT),
        name="proj_mixer_a",
    )(x, wa, wkw, conv_w, kg, kb)


def _sortable_to_f32(x):
    bits = x ^ ((x >> 31) & jnp.int32(0x7FFFFFFF))
    return pltpu.bitcast(bits, F32)


def _attn_kernel(qi_ref, wi_ref, ki_ref, q_ref, k_ref, vt_ref, yb_ref,
                 score_ref, bias_ref, qit_ref, qpad_ref, p_ref, m_ref, l_ref, acc_ref, *, k_sel):
    i = pl.program_id(1)
    nk = i + 1
    kf = float(k_sel)
    rep_heads = N_HEADS // N_KV_HEADS
    n_groups = TK // SUBLANES

    qit = qi_ref[0].astype(F32).T.astype(BF16)
    for h in range(IDX_HEADS):
        qit_ref[:, h * TQ:(h + 1) * TQ] = qit[h * IDX_DIM:(h + 1) * IDX_DIM, :]
    wt = wi_ref[0].T[:IDX_HEADS, :]
    qt = q_ref[0].astype(F32).T.astype(BF16)
    zero_half = jnp.zeros((HEAD_DIM, TQ), BF16)
    for h in range(N_HEADS):
        g = h // rep_heads
        for gg in range(N_KV_HEADS):
            qpad_ref[gg * HEAD_DIM:(gg + 1) * HEAD_DIM, h * TQ:(h + 1) * TQ] = (
                qt[h * HEAD_DIM:(h + 1) * HEAD_DIM, :] if gg == g else zero_half)

    row = lax.broadcasted_iota(jnp.int32, (TK, TQ), 0)
    col = lax.broadcasted_iota(jnp.int32, (TK, TQ), 1)
    row_minus_col = row - col

    def score_body(c, carry):
        kic = ki_ref[0, c]
        lg_all = jnp.dot(kic, qit_ref[...], preferred_element_type=F32)
        acc = jnp.zeros((TK, TQ), F32)
        for h in range(IDX_HEADS):
            lg = lg_all[:, h * TQ:(h + 1) * TQ]
            acc = acc + wt[h:h + 1, :] * jnp.maximum(lg, 0.0)
        noncausal = row_minus_col > (i - c) * TQ
        score_ref[c] = jnp.where(noncausal, NEG_INF, acc * IDX_SCALE)
        return carry

    lax.fori_loop(0, nk, score_body, 0)

    def count_where(pred):
        def body(c, acc):
            for r in range(n_groups):
                blk = score_ref[c, r * SUBLANES:(r + 1) * SUBLANES, :]
                acc = acc + jnp.where(pred(blk), 1.0, 0.0)
            return acc
        acc = lax.fori_loop(0, nk, body, jnp.zeros((SUBLANES, TQ), F32))
        return jnp.sum(acc, axis=0, keepdims=True)

    def count_ge(cand_key):
        cand = jnp.broadcast_to(_sortable_to_f32(cand_key), (SUBLANES, TQ))
        return count_where(lambda blk: blk >= cand)

    cnt0 = count_ge(jnp.zeros((1, TQ), jnp.int32))
    ok0 = cnt0 >= kf
    prefix0 = jnp.where(ok0, jnp.int32(0), jnp.int32(INT_MIN))
    cntp0 = jnp.where(ok0, cnt0, 0.0)

    def bit_body(b, carry):
        prefix, cntp = carry
        cand_key = prefix | jnp.left_shift(jnp.int32(1), 30 - b)
        cnt = count_ge(cand_key)
        ok = cnt >= kf
        return jnp.where(ok, cand_key, prefix), jnp.where(ok, cnt, cntp)

    prefix, cntp = lax.fori_loop(0, 31, bit_body, (prefix0, cntp0))
    tau = jnp.where(prefix == INT_MIN, -F32_MAX, _sortable_to_f32(prefix))

    def bias_body(c, carry):
        bias_ref[c] = jnp.where(score_ref[c] >= tau, 0.0, NEG_INF)
        return carry

    lax.fori_loop(0, nk, bias_body, 0)

    @pl.when(jnp.max(cntp) > kf)
    def _():
        tau8 = jnp.broadcast_to(tau, (SUBLANES, TQ))
        need = kf - count_where(lambda blk: blk > tau8)
        r = lax.broadcasted_iota(jnp.int32, (TK, TK), 0)
        cc = lax.broadcasted_iota(jnp.int32, (TK, TK), 1)
        before = jnp.where(cc < r, 1.0, 0.0).astype(BF16)

        def tie_body(c, seen):
            sc = score_ref[c]
            eq = sc == tau
            eqf = jnp.where(eq, 1.0, 0.0)
            rank = jnp.dot(before, eqf.astype(BF16), preferred_element_type=F32) + seen
            keep = (sc > tau) | (eq & (rank < need))
            bias_ref[c] = jnp.where(keep, 0.0, NEG_INF)
            return seen + jnp.sum(eqf, axis=0, keepdims=True)

        lax.fori_loop(0, nk, tie_body, jnp.zeros((1, TQ), F32))

    m_ref[...] = jnp.full(m_ref.shape, NEG_INF, F32)
    l_ref[...] = jnp.zeros(l_ref.shape, F32)
    acc_ref[...] = jnp.zeros(acc_ref.shape, F32)

    def att_body(c, carry):
        bias = bias_ref[c]
        kc = k_ref[0, c]
        s_all = jnp.dot(kc, qpad_ref[...], preferred_element_type=F32)
        alphas = []
        for h in range(N_HEADS):
            s = s_all[:, h * TQ:(h + 1) * TQ] + bias
            m_old = m_ref[h, 0:1, :]
            m_new = jnp.maximum(m_old, jnp.max(s, axis=0, keepdims=True))
            m_use = jnp.where(m_new == NEG_INF, 0.0, m_new)
            alpha = jnp.exp(m_old - m_use)
            p = jnp.exp(s - m_use)
            l_ref[h] = jnp.broadcast_to(alpha * l_ref[h, 0:1, :] + jnp.sum(p, axis=0, keepdims=True),
                                        (SUBLANES, TQ))
            m_ref[h] = jnp.broadcast_to(m_new, (SUBLANES, TQ))
            p_ref[:, h * TQ:(h + 1) * TQ] = p.astype(BF16)
            alphas.append(alpha)
        for g in range(N_KV_HEADS):
            lanes = slice(g * rep_heads * TQ, (g + 1) * rep_heads * TQ)
            pv = jnp.dot(vt_ref[0, c, g * HEAD_DIM:(g + 1) * HEAD_DIM, :], p_ref[:, lanes],
                         preferred_element_type=F32)
            for hh in range(rep_heads):
                h = g * rep_heads + hh
                rows = slice(h * HEAD_DIM, (h + 1) * HEAD_DIM)
                acc_ref[rows, :] = alphas[h] * acc_ref[rows, :] + pv[:, hh * TQ:(hh + 1) * TQ]
        return carry

    lax.fori_loop(0, nk, att_body, 0)

    for h in range(N_HEADS):
        rows = slice(h * HEAD_DIM, (h + 1) * HEAD_DIM)
        acc_ref[rows, :] = acc_ref[rows, :] / l_ref[h, 0:1, :]
    yb_ref[0] = acc_ref[...].T.astype(BF16)


def _run_attn(qi, wi, ki, q, k, vt, k_sel):
    b, s, _ = q.shape
    nc = s // TK
    ki = ki.reshape(b, nc, TK, IDX_DIM)
    k = k.reshape(b, nc, TK, D_KV)
    rows = lambda width: pl.BlockSpec((1, TQ, width), lambda bi, i: (bi, i, 0))
    keys = lambda d0, d1: pl.BlockSpec((1, nc, d0, d1), lambda bi, i: (bi, 0, 0, 0))
    return pl.pallas_call(
        functools.partial(_attn_kernel, k_sel=k_sel),
        grid=(b, s // TQ),
        in_specs=[rows(IDX_HEADS * IDX_DIM), rows(LANES), keys(TK, IDX_DIM),
                  rows(D_ATTN), keys(TK, D_KV), keys(D_KV, TK)],
        out_specs=rows(D_ATTN),
        out_shape=jax.ShapeDtypeStruct((b, s, D_ATTN), BF16),
        scratch_shapes=[pltpu.VMEM((nc, TK, TQ), F32),
                        pltpu.VMEM((nc, TK, TQ), F32),
                        pltpu.VMEM((IDX_DIM, IDX_HEADS * TQ), BF16),
                        pltpu.VMEM((D_KV, N_HEADS * TQ), BF16),
                        pltpu.VMEM((TK, N_HEADS * TQ), BF16),
                        pltpu.VMEM((N_HEADS, SUBLANES, TQ), F32),
                        pltpu.VMEM((N_HEADS, SUBLANES, TQ), F32),
                        pltpu.VMEM((D_ATTN, TQ), F32)],
        compiler_params=pltpu.CompilerParams(
            dimension_semantics=("parallel", "parallel"), vmem_limit_bytes=VMEM_LIMIT),
        name="dsa_attention",
    )(qi, wi, ki, q, k, vt)


def _tail_kernel(x_ref, ya_ref, yb_ref, wg_ref, wbr_ref, wo_ref, wup_ref, wdn_ref,
                 g1_ref, b1_ref, g2_ref, b2_ref, out_ref, *, alpha):
    x = x_ref[...]
    xb = x.astype(BF16)
    merged = jnp.zeros((TM_TAIL, D_MODEL), F32)
    for n, y_ref in enumerate((ya_ref, yb_ref)):
        gate = jnp.dot(xb, wg_ref[:, n * D_MODEL:(n + 1) * D_MODEL], preferred_element_type=F32)
        branch = jnp.dot(y_ref[...], wbr_ref[n], preferred_element_type=F32)
        merged = merged + jax.nn.sigmoid(gate) * branch
    mix = jnp.dot(merged.astype(BF16), wo_ref[...], preferred_element_type=F32)
    h = _layer_norm_rows(alpha * x + mix, g1_ref[...], b1_ref[...])
    hb = h.astype(BF16)
    ff = jnp.zeros((TM_TAIL, D_MODEL), F32)
    for f in range(D_FF // FF_CHUNK):
        up = jnp.dot(hb, wup_ref[:, f * FF_CHUNK:(f + 1) * FF_CHUNK], preferred_element_type=F32)
        act = jnp.square(jnp.maximum(up, 0.0)).astype(BF16)
        ff = ff + jnp.dot(act, wdn_ref[f * FF_CHUNK:(f + 1) * FF_CHUNK, :], preferred_element_type=F32)
    out_ref[...] = _layer_norm_rows(alpha * h + ff, g2_ref[...], b2_ref[...])


def _run_tail(x2, ya2, yb2, wg, wbr, wo, wup, wdn, g1, b1, g2, b2, alpha):
    n, d = x2.shape
    const = lambda *shape: pl.BlockSpec(shape, lambda r: (0,) * len(shape),
                                        pipeline_mode=pl.Buffered(1))
    rows = lambda width: pl.BlockSpec((TM_TAIL, width), lambda r: (r, 0))
    return pl.pallas_call(
        functools.partial(_tail_kernel, alpha=alpha),
        grid=(n // TM_TAIL,),
        in_specs=[rows(d), rows(D_CONV), rows(D_ATTN),
                  const(d, N_BRANCH * d), const(N_BRANCH, D_CONV, d), const(d, d),
                  const(d, D_FF), const(D_FF, d),
                  const(1, d), const(1, d), const(1, d), const(1, d)],
        out_specs=rows(d),
        out_shape=jax.ShapeDtypeStruct((n, d), F32),
        compiler_params=pltpu.CompilerParams(
            dimension_semantics=("parallel",), vmem_limit_bytes=VMEM_LIMIT),
        name="merge_mlp",
    )(x2, ya2, yb2, wg, wbr, wo, wup, wdn, g1, b1, g2, b2)


def kernel(x, w_in, conv_w, idx_k_norm_g, idx_k_norm_b, w_branch, w_o, ln1_g, ln1_b, w_up, w_down, ln2_g, ln2_b):
    b, s, d = x.shape
    depth = w_in.shape[0]
    assert d == D_MODEL and s % TM_PROJ == 0 and (b * s) % TM_TAIL == 0 and TQ == TK
    k_sel = min(TOPK_MAX, s // 4)
    alpha = (2 * depth) ** 0.25
    for l in range(depth):
        wl = w_in[l]
        wa = wl[:, :OFF_KI].astype(BF16)
        wkw = jnp.concatenate(
            [wl[:, OFF_KI:OFF_WI], jnp.zeros((d, LANES - IDX_DIM), F32),
             wl[:, OFF_WI:OFF_GATES], jnp.zeros((d, LANES - IDX_HEADS), F32)], axis=1).astype(BF16)
        wg = wl[:, OFF_GATES:].astype(BF16)
        pad = lambda a: jnp.pad(a.astype(F32), (0, LANES - IDX_DIM)).reshape(1, LANES)
        ya, q, k, vt, qi, ki, wi = _run_proj(x, wa, wkw, conv_w[l], pad(idx_k_norm_g[l]), pad(idx_k_norm_b[l]))
        yb = _run_attn(qi, wi, ki, q, k, vt, k_sel)
        row = lambda a: a.astype(F32).reshape(1, d)
        out = _run_tail(x.reshape(b * s, d), ya.reshape(b * s, D_CONV), yb.reshape(b * s, D_ATTN),
                        wg, w_branch[l].astype(BF16), w_o[l].astype(BF16),
                        w_up[l].astype(BF16), w_down[l].astype(BF16),
                        row(ln1_g[l]), row(ln1_b[l]), row(ln2_g[l]), row(ln2_b[l]), alpha)
        x = out.reshape(b, s, d)
    return x
```

```python
import functools

import jax
import jax.numpy as jnp
from jax import lax
from jax.experimental import pallas as pl
from jax.experimental.pallas import tpu as pltpu

D_MODEL = 1024
D_CONV = 512
CONV_WIDTH = 3
N_HEADS = 8
N_KV_HEADS = 2
HEAD_DIM = 64
D_ATTN = N_HEADS * HEAD_DIM
D_KV = N_KV_HEADS * HEAD_DIM
IDX_HEADS = 8
IDX_DIM = 64
TOPK_MAX = 256
N_BRANCH = 2
D_FF = 4 * D_MODEL
LN_EPS = 1e-5
IDX_SCALE = (IDX_DIM ** -0.5) * (IDX_HEADS ** -0.5)
ATTN_SCALE = HEAD_DIM ** -0.5

OFF_Q = 3 * D_CONV
OFF_K = OFF_Q + D_ATTN
OFF_V = OFF_K + D_KV
OFF_QI = OFF_V + D_KV
OFF_KI = OFF_QI + IDX_HEADS * IDX_DIM
OFF_WI = OFF_KI + IDX_DIM
OFF_GATES = OFF_WI + IDX_HEADS

LANES = 128
SUBLANES = 8
BF16_SUBLANES = 16
VMEM_LIMIT = 56 * 1024 * 1024

TM_PROJ = 512
TQ = 256
TK = 256
TM_TAIL = 512
FF_CHUNK = 1024
COUNT_CHAINS = 4

F32 = jnp.float32
BF16 = jnp.bfloat16
NEG_INF = float("-inf")
F32_MAX = float(jnp.finfo(jnp.float32).max)
INT_MIN = -2 ** 31


def _layer_norm_rows(x, g, b):
    mu = jnp.mean(x, axis=-1, keepdims=True)
    xc = x - mu
    var = jnp.mean(xc * xc, axis=-1, keepdims=True)
    return xc * lax.rsqrt(var + LN_EPS) * g + b


def _proj_kernel(x_ref, wa_ref, wkw_ref, convw_ref, kg_ref, kb_ref,
                 ya_ref, q_ref, k_ref, vt_ref, qi_ref, ki_ref, wi_ref,
                 carry_ref):
    j = pl.program_id(1)
    xb = x_ref[0].astype(BF16)

    def proj(lo, hi):
        return jnp.dot(xb, wa_ref[:, lo:hi], preferred_element_type=F32)

    cu = proj(D_CONV, 2 * D_CONV) * proj(2 * D_CONV, 3 * D_CONV)

    @pl.when(j == 0)
    def _():
        carry_ref[...] = jnp.zeros_like(carry_ref)

    prev = carry_ref[...]
    row = lax.broadcasted_iota(jnp.int32, cu.shape, 0)
    cu1 = jnp.where(row == 0, prev[SUBLANES - 1:SUBLANES, :], pltpu.roll(cu, 1, axis=0))
    cu2 = jnp.where(row == 0, prev[SUBLANES - 2:SUBLANES - 1, :],
                    jnp.where(row == 1, prev[SUBLANES - 1:SUBLANES, :], pltpu.roll(cu, 2, axis=0)))
    carry_ref[...] = cu[TM_PROJ - SUBLANES:, :]
    cw = convw_ref[...]
    conv = cw[0:1, :] * cu2 + cw[1:2, :] * cu1 + cw[2:3, :] * cu
    ya_ref[0] = (proj(0, D_CONV) * conv).astype(BF16)

    q_ref[0] = (proj(OFF_Q, OFF_K) * ATTN_SCALE).astype(BF16)
    k_ref[0] = proj(OFF_K, OFF_V).astype(BF16)
    vt = proj(OFF_V, OFF_QI).T
    for c in range(TM_PROJ // TK):
        vt_ref[0, c] = vt[:, c * TK:(c + 1) * TK].astype(BF16)
    qi_ref[0] = proj(OFF_QI, OFF_KI).astype(BF16)

    zkw = jnp.dot(xb, wkw_ref[...], preferred_element_type=F32)
    wi_ref[0] = zkw[:, LANES:]
    zk = zkw[:, :LANES]
    valid = lax.broadcasted_iota(jnp.int32, zk.shape, 1) < IDX_DIM
    mu = jnp.sum(jnp.where(valid, zk, 0.0), axis=-1, keepdims=True) * (1.0 / IDX_DIM)
    xc = jnp.where(valid, zk - mu, 0.0)
    var = jnp.sum(xc * xc, axis=-1, keepdims=True) * (1.0 / IDX_DIM)
    kin = xc * lax.rsqrt(var + LN_EPS) * kg_ref[...] + kb_ref[...]
    ki_ref[0] = kin[:, :IDX_DIM].astype(BF16)


def _run_proj(x, wa, wkw, conv_w, kg, kb):
    b, s, d = x.shape
    nc = s // TK
    cpt = TM_PROJ // TK
    const = lambda *shape: pl.BlockSpec(shape, lambda bi, j: (0,) * len(shape),
                                        pipeline_mode=pl.Buffered(1))
    rows = lambda width: pl.BlockSpec((1, TM_PROJ, width), lambda bi, j: (bi, j, 0))
    return pl.pallas_call(
        _proj_kernel,
        grid=(b, s // TM_PROJ),
        in_specs=[rows(d), const(d, OFF_KI), const(d, 2 * LANES), const(CONV_WIDTH, D_CONV),
                  const(1, LANES), const(1, LANES)],
        out_specs=[rows(D_CONV), rows(D_ATTN), rows(D_KV),
                   pl.BlockSpec((1, cpt, D_KV, TK), lambda bi, j: (bi, j, 0, 0)),
                   rows(IDX_HEADS * IDX_DIM), rows(IDX_DIM), rows(LANES)],
        out_shape=[jax.ShapeDtypeStruct((b, s, D_CONV), BF16),
                   jax.ShapeDtypeStruct((b, s, D_ATTN), BF16),
                   jax.ShapeDtypeStruct((b, s, D_KV), BF16),
                   jax.ShapeDtypeStruct((b, nc, D_KV, TK), BF16),
                   jax.ShapeDtypeStruct((b, s, IDX_HEADS * IDX_DIM), BF16),
                   jax.ShapeDtypeStruct((b, s, IDX_DIM), BF16),
                   jax.ShapeDtypeStruct((b, s, LANES), F32)],
        scratch_shapes=[pltpu.VMEM((SUBLANES, D_CONV), F32)],
        compiler_params=pltpu.CompilerParams(
            dimension_semantics=("parallel", "arbitrary"), vmem_limit_bytes=VMEM_LIMIT),
        name="proj_mixer_a",
    )(x, wa, wkw, conv_w, kg, kb)


def _sortable_to_f32(x):
    bits = x ^ ((x >> 31) & jnp.int32(0x7FFFFFFF))
    return pltpu.bitcast(bits, F32)


def _attn_kernel(qi_ref, wi_ref, ki_ref, q_ref, k_ref, vt_ref, yb_ref,
                 score_ref, bias_ref, qit_ref, qpad_ref, s0_ref, s1_ref, p0_ref, p1_ref,
                 mc0_ref, mc1_ref, alpha_ref, m_ref, l_ref, acc_ref, *, k_sel):
    i = pl.program_id(1)
    nk = i + 1
    kf = float(k_sel)
    rep_heads = N_HEADS // N_KV_HEADS
    n_groups = TK // SUBLANES

    qit = qi_ref[0].astype(F32).T.astype(BF16)
    for h in range(IDX_HEADS):
        qit_ref[:, h * TQ:(h + 1) * TQ] = qit[h * IDX_DIM:(h + 1) * IDX_DIM, :]
    wt = wi_ref[0].T[:IDX_HEADS, :]
    qt = q_ref[0].astype(F32).T.astype(BF16)
    zero_half = jnp.zeros((HEAD_DIM, TQ), BF16)
    for h in range(N_HEADS):
        g = h // rep_heads
        for gg in range(N_KV_HEADS):
            qpad_ref[gg * HEAD_DIM:(gg + 1) * HEAD_DIM, h * TQ:(h + 1) * TQ] = (
                qt[h * HEAD_DIM:(h + 1) * HEAD_DIM, :] if gg == g else zero_half)

    row = lax.broadcasted_iota(jnp.int32, (TK, TQ), 0)
    col = lax.broadcasted_iota(jnp.int32, (TK, TQ), 1)
    row_minus_col = row - col

    def score_body(c, carry):
        kic = ki_ref[0, c]
        lg_all = jnp.dot(kic, qit_ref[...], preferred_element_type=F32)
        acc = jnp.zeros((TK, TQ), F32)
        for h in range(IDX_HEADS):
            lg = lg_all[:, h * TQ:(h + 1) * TQ]
            acc = acc + wt[h:h + 1, :] * jnp.maximum(lg, 0.0)
        noncausal = row_minus_col > (i - c) * TQ
        score_ref[c] = jnp.where(noncausal, NEG_INF, acc * IDX_SCALE)
        return carry

    lax.fori_loop(0, nk, score_body, 0)

    def count_where(pred):
        def body(c, accs):
            accs = list(accs)
            for r in range(n_groups):
                blk = score_ref[c, r * SUBLANES:(r + 1) * SUBLANES, :]
                accs[r % COUNT_CHAINS] = accs[r % COUNT_CHAINS] + jnp.where(pred(blk), 1.0, 0.0)
            return tuple(accs)
        accs = lax.fori_loop(0, nk, body, (jnp.zeros((SUBLANES, TQ), F32),) * COUNT_CHAINS)
        return jnp.sum(sum(accs[1:], accs[0]), axis=0, keepdims=True)

    def count_ge(cand_key):
        cand = jnp.broadcast_to(_sortable_to_f32(cand_key), (SUBLANES, TQ))
        return count_where(lambda blk: blk >= cand)

    cnt0 = count_ge(jnp.zeros((1, TQ), jnp.int32))
    ok0 = cnt0 >= kf
    prefix0 = jnp.where(ok0, jnp.int32(0), jnp.int32(INT_MIN))
    cntp0 = jnp.where(ok0, cnt0, 0.0)

    def bit_body(b, carry):
        prefix, cntp = carry
        cand_key = prefix | jnp.left_shift(jnp.int32(1), 30 - b)
        cnt = count_ge(cand_key)
        ok = cnt >= kf
        return jnp.where(ok, cand_key, prefix), jnp.where(ok, cnt, cntp)

    prefix, cntp = lax.fori_loop(0, 31, bit_body, (prefix0, cntp0))
    tau = jnp.where(prefix == INT_MIN, -F32_MAX, _sortable_to_f32(prefix))

    def bias_body(c, carry):
        bias_ref[c] = jnp.where(score_ref[c] >= tau, 0.0, NEG_INF)
        return carry

    lax.fori_loop(0, nk, bias_body, 0)

    @pl.when(jnp.max(cntp) > kf)
    def _():
        tau8 = jnp.broadcast_to(tau, (SUBLANES, TQ))
        need = kf - count_where(lambda blk: blk > tau8)
        r = lax.broadcasted_iota(jnp.int32, (TK, TK), 0)
        cc = lax.broadcasted_iota(jnp.int32, (TK, TK), 1)
        before = jnp.where(cc < r, 1.0, 0.0).astype(BF16)

        def tie_body(c, seen):
            sc = score_ref[c]
            eq = sc == tau
            eqf = jnp.where(eq, 1.0, 0.0)
            rank = jnp.dot(before, eqf.astype(BF16), preferred_element_type=F32) + seen
            keep = (sc > tau) | (eq & (rank < need))
            bias_ref[c] = jnp.where(keep, 0.0, NEG_INF)
            return seen + jnp.sum(eqf, axis=0, keepdims=True)

        lax.fori_loop(0, nk, tie_body, jnp.zeros((1, TQ), F32))

    m_ref[...] = jnp.full(m_ref.shape, NEG_INF, F32)
    l_ref[...] = jnp.zeros(l_ref.shape, F32)
    acc_ref[...] = jnp.zeros(acc_ref.shape, F32)
    alpha_ref[...] = jnp.ones(alpha_ref.shape, F32)
    p1_ref[...] = jnp.zeros(p1_ref.shape, BF16)
    ones_rows = jnp.ones((BF16_SUBLANES, TK), BF16)

    def logits_head(h, kc, bias, s_out, mc_out):
        cols = slice(h * TQ, (h + 1) * TQ)
        s = jnp.dot(kc, qpad_ref[:, cols], preferred_element_type=F32) + bias
        s_out[:, cols] = s
        mc_out[h] = jnp.broadcast_to(jnp.max(s, axis=0, keepdims=True), (SUBLANES, TQ))

    def pv_update(c, p_in):
        for g in range(N_KV_HEADS):
            lanes = slice(g * rep_heads * TQ, (g + 1) * rep_heads * TQ)
            lhs = jnp.concatenate([vt_ref[0, c, g * HEAD_DIM:(g + 1) * HEAD_DIM, :], ones_rows], axis=0)
            pv = jnp.dot(lhs, p_in[:, lanes], preferred_element_type=F32)
            for hh in range(rep_heads):
                h = g * rep_heads + hh
                rows = slice(h * HEAD_DIM, (h + 1) * HEAD_DIM)
                cols = slice(hh * TQ, (hh + 1) * TQ)
                alpha = alpha_ref[h, 0:1, :]
                acc_ref[rows, :] = alpha * acc_ref[rows, :] + pv[:HEAD_DIM, cols]
                l_ref[h] = jnp.broadcast_to(alpha * l_ref[h, 0:1, :] + pv[HEAD_DIM:HEAD_DIM + 1, cols],
                                            (SUBLANES, TQ))

    def softmax_head(h, s_in, mc_in, p_out):
        m_old = m_ref[h, 0:1, :]
        m_new = jnp.maximum(m_old, mc_in[h, 0:1, :])
        m_use = jnp.where(m_new == NEG_INF, 0.0, m_new)
        alpha_ref[h] = jnp.broadcast_to(jnp.exp(m_old - m_use), (SUBLANES, TQ))
        m_ref[h] = jnp.broadcast_to(m_new, (SUBLANES, TQ))
        p_out[:, h * TQ:(h + 1) * TQ] = jnp.exp(s_in[:, h * TQ:(h + 1) * TQ] - m_use).astype(BF16)

    def stage(c, s_cur, mc_cur, p_cur, s_nxt, mc_nxt, p_prev):
        c_nxt = jnp.minimum(c + 1, nk - 1)
        pv_update(jnp.maximum(c - 1, 0), p_prev)
        bias_nxt = bias_ref[c_nxt]
        k_nxt = k_ref[0, c_nxt]
        for h in range(N_HEADS):
            logits_head(h, k_nxt, bias_nxt, s_nxt, mc_nxt)
            softmax_head(h, s_cur, mc_cur, p_cur)

    for h in range(N_HEADS):
        logits_head(h, k_ref[0, 0], bias_ref[0], s0_ref, mc0_ref)

    def pair_body(j, carry):
        c = 2 * j
        stage(c, s0_ref, mc0_ref, p0_ref, s1_ref, mc1_ref, p1_ref)

        @pl.when(c + 1 < nk)
        def _():
            stage(c + 1, s1_ref, mc1_ref, p1_ref, s0_ref, mc0_ref, p0_ref)

        return carry

    lax.fori_loop(0, (nk + 1) // 2, pair_body, 0)

    @pl.when((nk - 1) % 2 == 0)
    def _():
        pv_update(nk - 1, p0_ref)

    @pl.when((nk - 1) % 2 == 1)
    def _():
        pv_update(nk - 1, p1_ref)

    for h in range(N_HEADS):
        rows = slice(h * HEAD_DIM, (h + 1) * HEAD_DIM)
        acc_ref[rows, :] = acc_ref[rows, :] / l_ref[h, 0:1, :]
    yb_ref[0] = acc_ref[...].T.astype(BF16)


def _run_attn(qi, wi, ki, q, k, vt, k_sel):
    b, s, _ = q.shape
    nc = s // TK
    ki = ki.reshape(b, nc, TK, IDX_DIM)
    k = k.reshape(b, nc, TK, D_KV)
    rows = lambda width: pl.BlockSpec((1, TQ, width), lambda bi, i: (bi, i, 0))
    keys = lambda d0, d1: pl.BlockSpec((1, nc, d0, d1), lambda bi, i: (bi, 0, 0, 0))
    return pl.pallas_call(
        functools.partial(_attn_kernel, k_sel=k_sel),
        grid=(b, s // TQ),
        in_specs=[rows(IDX_HEADS * IDX_DIM), rows(LANES), keys(TK, IDX_DIM),
                  rows(D_ATTN), keys(TK, D_KV), keys(D_KV, TK)],
        out_specs=rows(D_ATTN),
        out_shape=jax.ShapeDtypeStruct((b, s, D_ATTN), BF16),
        scratch_shapes=[pltpu.VMEM((nc, TK, TQ), F32),
                        pltpu.VMEM((nc, TK, TQ), F32),
                        pltpu.VMEM((IDX_DIM, IDX_HEADS * TQ), BF16),
                        pltpu.VMEM((D_KV, N_HEADS * TQ), BF16),
                        pltpu.VMEM((TK, N_HEADS * TQ), F32),
                        pltpu.VMEM((TK, N_HEADS * TQ), F32),
                        pltpu.VMEM((TK, N_HEADS * TQ), BF16),
                        pltpu.VMEM((TK, N_HEADS * TQ), BF16),
                        pltpu.VMEM((N_HEADS, SUBLANES, TQ), F32),
                        pltpu.VMEM((N_HEADS, SUBLANES, TQ), F32),
                        pltpu.VMEM((N_HEADS, SUBLANES, TQ), F32),
                        pltpu.VMEM((N_HEADS, SUBLANES, TQ), F32),
                        pltpu.VMEM((N_HEADS, SUBLANES, TQ), F32),
                        pltpu.VMEM((D_ATTN, TQ), F32)],
        compiler_params=pltpu.CompilerParams(
            dimension_semantics=("parallel", "parallel"), vmem_limit_bytes=VMEM_LIMIT),
        name="dsa_attention",
    )(qi, wi, ki, q, k, vt)


def _tail_kernel(x_ref, ya_ref, yb_ref, wg_ref, wbr_ref, wo_ref, wup_ref, wdn_ref,
                 g1_ref, b1_ref, g2_ref, b2_ref, out_ref, *, alpha):
    x = x_ref[...]
    xb = x.astype(BF16)
    merged = jnp.zeros((TM_TAIL, D_MODEL), F32)
    for n, y_ref in enumerate((ya_ref, yb_ref)):
        gate = jnp.dot(xb, wg_ref[:, n * D_MODEL:(n + 1) * D_MODEL], preferred_element_type=F32)
        branch = jnp.dot(y_ref[...], wbr_ref[n], preferred_element_type=F32)
        merged = merged + jax.nn.sigmoid(gate) * branch
    mix = jnp.dot(merged.astype(BF16), wo_ref[...], preferred_element_type=F32)
    h = _layer_norm_rows(alpha * x + mix, g1_ref[...], b1_ref[...])
    hb = h.astype(BF16)
    ff = jnp.zeros((TM_TAIL, D_MODEL), F32)
    for f in range(D_FF // FF_CHUNK):
        up = jnp.dot(hb, wup_ref[:, f * FF_CHUNK:(f + 1) * FF_CHUNK], preferred_element_type=F32)
        act = jnp.square(jnp.maximum(up, 0.0)).astype(BF16)
        ff = ff + jnp.dot(act, wdn_ref[f * FF_CHUNK:(f + 1) * FF_CHUNK, :], preferred_element_type=F32)
    out_ref[...] = _layer_norm_rows(alpha * h + ff, g2_ref[...], b2_ref[...])


def _run_tail(x2, ya2, yb2, wg, wbr, wo, wup, wdn, g1, b1, g2, b2, alpha):
    n, d = x2.shape
    const = lambda *shape: pl.BlockSpec(shape, lambda r: (0,) * len(shape),
                                        pipeline_mode=pl.Buffered(1))
    rows = lambda width: pl.BlockSpec((TM_TAIL, width), lambda r: (r, 0))
    return pl.pallas_call(
        functools.partial(_tail_kernel, alpha=alpha),
        grid=(n // TM_TAIL,),
        in_specs=[rows(d), rows(D_CONV), rows(D_ATTN),
                  const(d, N_BRANCH * d), const(N_BRANCH, D_CONV, d), const(d, d),
                  const(d, D_FF), const(D_FF, d),
                  const(1, d), const(1, d), const(1, d), const(1, d)],
        out_specs=rows(d),
        out_shape=jax.ShapeDtypeStruct((n, d), F32),
        compiler_params=pltpu.CompilerParams(
            dimension_semantics=("parallel",), vmem_limit_bytes=VMEM_LIMIT),
        name="merge_mlp",
    )(x2, ya2, yb2, wg, wbr, wo, wup, wdn, g1, b1, g2, b2)


def kernel(x, w_in, conv_w, idx_k_norm_g, idx_k_norm_b, w_branch, w_o, ln1_g, ln1_b, w_up, w_down, ln2_g, ln2_b):
    b, s, d = x.shape
    depth = w_in.shape[0]
    assert d == D_MODEL and s % TM_PROJ == 0 and (b * s) % TM_TAIL == 0 and TQ == TK
    k_sel = min(TOPK_MAX, s // 4)
    alpha = (2 * depth) ** 0.25
    for l in range(depth):
        wl = w_in[l]
        wa = wl[:, :OFF_KI].astype(BF16)
        wkw = jnp.concatenate(
            [wl[:, OFF_KI:OFF_WI], jnp.zeros((d, LANES - IDX_DIM), F32),
             wl[:, OFF_WI:OFF_GATES], jnp.zeros((d, LANES - IDX_HEADS), F32)], axis=1).astype(BF16)
        wg = wl[:, OFF_GATES:].astype(BF16)
        pad = lambda a: jnp.pad(a.astype(F32), (0, LANES - IDX_DIM)).reshape(1, LANES)
        ya, q, k, vt, qi, ki, wi = _run_proj(x, wa, wkw, conv_w[l], pad(idx_k_norm_g[l]), pad(idx_k_norm_b[l]))
        yb = _run_attn(qi, wi, ki, q, k, vt, k_sel)
        row = lambda a: a.astype(F32).reshape(1, d)
        out = _run_tail(x.reshape(b * s, d), ya.reshape(b * s, D_CONV), yb.reshape(b * s, D_ATTN),
                        wg, w_branch[l].astype(BF16), w_o[l].astype(BF16),
                        w_up[l].astype(BF16), w_down[l].astype(BF16),
                        row(ln1_g[l]), row(ln1_b[l]), row(ln2_g[l]), row(ln2_b[l]), alpha)
        x = out.reshape(b, s, d)
    return x
```

```python
import functools

import jax
import jax.numpy as jnp
from jax import lax
from jax.experimental import pallas as pl
from jax.experimental.pallas import tpu as pltpu

D_MODEL = 1024
D_CONV = 512
CONV_WIDTH = 3
N_HEADS = 8
N_KV_HEADS = 2
HEAD_DIM = 64
D_ATTN = N_HEADS * HEAD_DIM
D_KV = N_KV_HEADS * HEAD_DIM
IDX_HEADS = 8
IDX_DIM = 64
TOPK_MAX = 256
N_BRANCH = 2
D_FF = 4 * D_MODEL
LN_EPS = 1e-5
IDX_SCALE = (IDX_DIM ** -0.5) * (IDX_HEADS ** -0.5)
ATTN_SCALE = HEAD_DIM ** -0.5
LOG2_E = 1.4426950408889634
Q_SCALE = ATTN_SCALE * LOG2_E

OFF_Q = 3 * D_CONV
OFF_K = OFF_Q + D_ATTN
OFF_V = OFF_K + D_KV
OFF_QI = OFF_V + D_KV
OFF_KI = OFF_QI + IDX_HEADS * IDX_DIM
OFF_WI = OFF_KI + IDX_DIM
OFF_GATES = OFF_WI + IDX_HEADS

LANES = 128
SUBLANES = 8
BF16_SUBLANES = 16
VMEM_LIMIT = 56 * 1024 * 1024

TM_PROJ = 512
TQ = 256
TK = 256
TM_TAIL = 512
FF_CHUNK = 1024
COUNT_CHAINS = 4
EXP_ROWS = 32

F32 = jnp.float32
BF16 = jnp.bfloat16
NEG_INF = float("-inf")
F32_MAX = float(jnp.finfo(jnp.float32).max)
INT_MIN = -2 ** 31


def _layer_norm_rows(x, g, b):
    mu = jnp.mean(x, axis=-1, keepdims=True)
    xc = x - mu
    var = jnp.mean(xc * xc, axis=-1, keepdims=True)
    return xc * lax.rsqrt(var + LN_EPS) * g + b


def _proj_kernel(x_ref, wa_ref, wkw_ref, convw_ref, kg_ref, kb_ref,
                 ya_ref, q_ref, k_ref, vt_ref, qi_ref, ki_ref, wi_ref,
                 carry_ref):
    @pl.when(pl.program_id(1) == 0)
    def _():
        carry_ref[...] = jnp.zeros_like(carry_ref)

    xb = x_ref[0].astype(BF16)

    def proj(lo, hi):
        return jnp.dot(xb, wa_ref[:, lo:hi], preferred_element_type=F32)

    zkw = jnp.dot(xb, wkw_ref[...], preferred_element_type=F32)
    wi_ref[0] = zkw[:, LANES:]
    zk = zkw[:, :LANES]
    valid = lax.broadcasted_iota(jnp.int32, zk.shape, 1) < IDX_DIM
    mu = jnp.sum(jnp.where(valid, zk, 0.0), axis=-1, keepdims=True) * (1.0 / IDX_DIM)
    xc = jnp.where(valid, zk - mu, 0.0)
    var = jnp.sum(xc * xc, axis=-1, keepdims=True) * (1.0 / IDX_DIM)
    kin = xc * lax.rsqrt(var + LN_EPS) * kg_ref[...] + kb_ref[...]
    ki_ref[0] = kin[:, :IDX_DIM].astype(BF16)

    vt = proj(OFF_V, OFF_QI).T
    for c in range(TM_PROJ // TK):
        vt_ref[0, c] = vt[:, c * TK:(c + 1) * TK].astype(BF16)

    cu = proj(D_CONV, 2 * D_CONV) * proj(2 * D_CONV, 3 * D_CONV)
    prev = carry_ref[...]
    row = lax.broadcasted_iota(jnp.int32, cu.shape, 0)
    cu1 = jnp.where(row == 0, prev[SUBLANES - 1:SUBLANES, :], pltpu.roll(cu, 1, axis=0))
    cu2 = jnp.where(row == 0, prev[SUBLANES - 2:SUBLANES - 1, :],
                    jnp.where(row == 1, prev[SUBLANES - 1:SUBLANES, :], pltpu.roll(cu, 2, axis=0)))
    carry_ref[...] = cu[TM_PROJ - SUBLANES:, :]
    cw = convw_ref[...]
    conv = cw[0:1, :] * cu2 + cw[1:2, :] * cu1 + cw[2:3, :] * cu
    ya_ref[0] = (proj(0, D_CONV) * conv).astype(BF16)

    k_ref[0] = proj(OFF_K, OFF_V).astype(BF16)
    q_ref[0] = (proj(OFF_Q, OFF_K) * Q_SCALE).astype(BF16)
    qi_ref[0] = proj(OFF_QI, OFF_KI).astype(BF16)


def _run_proj(x, wa, wkw, conv_w, kg, kb):
    b, s, d = x.shape
    nc = s // TK
    cpt = TM_PROJ // TK
    const = lambda *shape: pl.BlockSpec(shape, lambda bi, j: (0,) * len(shape),
                                        pipeline_mode=pl.Buffered(1))
    rows = lambda width: pl.BlockSpec((1, TM_PROJ, width), lambda bi, j: (bi, j, 0))
    return pl.pallas_call(
        _proj_kernel,
        grid=(b, s // TM_PROJ),
        in_specs=[rows(d), const(d, OFF_KI), const(d, 2 * LANES), const(CONV_WIDTH, D_CONV),
                  const(1, LANES), const(1, LANES)],
        out_specs=[rows(D_CONV), rows(D_ATTN), rows(D_KV),
                   pl.BlockSpec((1, cpt, D_KV, TK), lambda bi, j: (bi, j, 0, 0)),
                   rows(IDX_HEADS * IDX_DIM), rows(IDX_DIM), rows(LANES)],
        out_shape=[jax.ShapeDtypeStruct((b, s, D_CONV), BF16),
                   jax.ShapeDtypeStruct((b, s, D_ATTN), BF16),
                   jax.ShapeDtypeStruct((b, s, D_KV), BF16),
                   jax.ShapeDtypeStruct((b, nc, D_KV, TK), BF16),
                   jax.ShapeDtypeStruct((b, s, IDX_HEADS * IDX_DIM), BF16),
                   jax.ShapeDtypeStruct((b, s, IDX_DIM), BF16),
                   jax.ShapeDtypeStruct((b, s, LANES), F32)],
        scratch_shapes=[pltpu.VMEM((SUBLANES, D_CONV), F32)],
        compiler_params=pltpu.CompilerParams(
            dimension_semantics=("parallel", "arbitrary"), vmem_limit_bytes=VMEM_LIMIT),
        name="proj_mixer_a",
    )(x, wa, wkw, conv_w, kg, kb)


def _sortable_to_f32(x):
    bits = x ^ ((x >> 31) & jnp.int32(0x7FFFFFFF))
    return pltpu.bitcast(bits, F32)


def _attn_kernel(qi_ref, wi_ref, ki_ref, q_ref, k_ref, vt_ref, yb_ref,
                 score_ref, bias_ref, qit_ref, qpad_ref, s0_ref, s1_ref, p0_ref, p1_ref,
                 mc0_ref, mc1_ref, alpha_ref, m_ref, l_ref, acc_ref, *, k_sel):
    i = pl.program_id(1)
    nk = i + 1
    kf = float(k_sel)
    rep_heads = N_HEADS // N_KV_HEADS
    n_groups = TK // SUBLANES

    qit = qi_ref[0].astype(F32).T.astype(BF16)
    for h in range(IDX_HEADS):
        qit_ref[:, h * TQ:(h + 1) * TQ] = qit[h * IDX_DIM:(h + 1) * IDX_DIM, :]
    wt = wi_ref[0].T[:IDX_HEADS, :]
    qt = q_ref[0].astype(F32).T.astype(BF16)
    zero_half = jnp.zeros((HEAD_DIM, TQ), BF16)
    for h in range(N_HEADS):
        g = h // rep_heads
        for gg in range(N_KV_HEADS):
            qpad_ref[gg * HEAD_DIM:(gg + 1) * HEAD_DIM, h * TQ:(h + 1) * TQ] = (
                qt[h * HEAD_DIM:(h + 1) * HEAD_DIM, :] if gg == g else zero_half)

    row = lax.broadcasted_iota(jnp.int32, (TK, TQ), 0)
    col = lax.broadcasted_iota(jnp.int32, (TK, TQ), 1)
    row_minus_col = row - col

    def score_body(c, carry):
        kic = ki_ref[0, c]
        lg_all = jnp.dot(kic, qit_ref[...], preferred_element_type=F32)
        acc = jnp.zeros((TK, TQ), F32)
        for h in range(IDX_HEADS):
            lg = lg_all[:, h * TQ:(h + 1) * TQ]
            acc = acc + wt[h:h + 1, :] * jnp.maximum(lg, 0.0)
        noncausal = row_minus_col > (i - c) * TQ
        score_ref[c] = jnp.where(noncausal, NEG_INF, acc * IDX_SCALE)
        return carry

    lax.fori_loop(0, nk, score_body, 0)

    def count_where(pred):
        def body(c, accs):
            accs = list(accs)
            for r in range(n_groups):
                blk = score_ref[c, r * SUBLANES:(r + 1) * SUBLANES, :]
                accs[r % COUNT_CHAINS] = accs[r % COUNT_CHAINS] + jnp.where(pred(blk), 1.0, 0.0)
            return tuple(accs)
        accs = lax.fori_loop(0, nk, body, (jnp.zeros((SUBLANES, TQ), F32),) * COUNT_CHAINS)
        return jnp.sum(sum(accs[1:], accs[0]), axis=0, keepdims=True)

    def count_ge(cand_key):
        cand = jnp.broadcast_to(_sortable_to_f32(cand_key), (SUBLANES, TQ))
        return count_where(lambda blk: blk >= cand)

    cnt0 = count_ge(jnp.zeros((1, TQ), jnp.int32))
    ok0 = cnt0 >= kf
    prefix0 = jnp.where(ok0, jnp.int32(0), jnp.int32(INT_MIN))
    cntp0 = jnp.where(ok0, cnt0, 0.0)

    def bit_body(b, carry):
        prefix, cntp = carry
        cand_key = prefix | jnp.left_shift(jnp.int32(1), 30 - b)
        cnt = count_ge(cand_key)
        ok = cnt >= kf
        return jnp.where(ok, cand_key, prefix), jnp.where(ok, cnt, cntp)

    prefix, cntp = lax.fori_loop(0, 31, bit_body, (prefix0, cntp0))
    tau = jnp.where(prefix == INT_MIN, -F32_MAX, _sortable_to_f32(prefix))

    def bias_body(c, carry):
        bias_ref[c] = jnp.where(score_ref[c] >= tau, 0.0, NEG_INF)
        return carry

    lax.fori_loop(0, nk, bias_body, 0)

    @pl.when(jnp.max(cntp) > kf)
    def _():
        tau8 = jnp.broadcast_to(tau, (SUBLANES, TQ))
        need = kf - count_where(lambda blk: blk > tau8)
        r = lax.broadcasted_iota(jnp.int32, (TK, TK), 0)
        cc = lax.broadcasted_iota(jnp.int32, (TK, TK), 1)
        before = jnp.where(cc < r, 1.0, 0.0).astype(BF16)

        def tie_body(c, seen):
            sc = score_ref[c]
            eq = sc == tau
            eqf = jnp.where(eq, 1.0, 0.0)
            rank = jnp.dot(before, eqf.astype(BF16), preferred_element_type=F32) + seen
            keep = (sc > tau) | (eq & (rank < need))
            bias_ref[c] = jnp.where(keep, 0.0, NEG_INF)
            return seen + jnp.sum(eqf, axis=0, keepdims=True)

        lax.fori_loop(0, nk, tie_body, jnp.zeros((1, TQ), F32))

    m_ref[...] = jnp.full(m_ref.shape, NEG_INF, F32)
    l_ref[...] = jnp.zeros(l_ref.shape, F32)
    acc_ref[...] = jnp.zeros(acc_ref.shape, F32)
    alpha_ref[...] = jnp.ones(alpha_ref.shape, F32)
    p1_ref[...] = jnp.zeros(p1_ref.shape, BF16)
    ones_rows = jnp.ones((BF16_SUBLANES, TK), BF16)

    def logits_head(h, kc, bias, s_out, mc_out):
        cols = slice(h * TQ, (h + 1) * TQ)
        s = jnp.dot(kc, qpad_ref[:, cols], preferred_element_type=F32) + bias
        s_out[:, cols] = s
        mc_out[h] = jnp.broadcast_to(jnp.max(s, axis=0, keepdims=True), (SUBLANES, TQ))

    def pv_update(c, p_in):
        for g in range(N_KV_HEADS):
            lanes = slice(g * rep_heads * TQ, (g + 1) * rep_heads * TQ)
            lhs = jnp.concatenate([vt_ref[0, c, g * HEAD_DIM:(g + 1) * HEAD_DIM, :], ones_rows], axis=0)
            pv = jnp.dot(lhs, p_in[:, lanes], preferred_element_type=F32)
            for hh in range(rep_heads):
                h = g * rep_heads + hh
                rows = slice(h * HEAD_DIM, (h + 1) * HEAD_DIM)
                cols = slice(hh * TQ, (hh + 1) * TQ)
                alpha = alpha_ref[h, 0:1, :]
                acc_ref[rows, :] = alpha * acc_ref[rows, :] + pv[:HEAD_DIM, cols]
                l_ref[h] = jnp.broadcast_to(alpha * l_ref[h, 0:1, :] + pv[HEAD_DIM:HEAD_DIM + 1, cols],
                                            (SUBLANES, TQ))

    def softmax_head(h, s_in, mc_in, p_out):
        m_old = m_ref[h, 0:1, :]
        m_new = jnp.maximum(m_old, mc_in[h, 0:1, :])
        m_use = jnp.where(m_new == NEG_INF, 0.0, m_new)
        alpha_ref[h] = jnp.broadcast_to(jnp.exp2(m_old - m_use), (SUBLANES, TQ))
        m_ref[h] = jnp.broadcast_to(m_new, (SUBLANES, TQ))
        cols = slice(h * TQ, (h + 1) * TQ)
        for blk in range(TK // EXP_ROWS):
            rows = slice(blk * EXP_ROWS, (blk + 1) * EXP_ROWS)
            p_out[rows, cols] = jnp.exp2(s_in[rows, cols] - m_use).astype(BF16)

    def stage(c, s_cur, mc_cur, p_cur, s_nxt, mc_nxt, p_prev):
        c_nxt = jnp.minimum(c + 1, nk - 1)
        pv_update(jnp.maximum(c - 1, 0), p_prev)
        bias_nxt = bias_ref[c_nxt]
        k_nxt = k_ref[0, c_nxt]
        for h in range(N_HEADS):
            logits_head(h, k_nxt, bias_nxt, s_nxt, mc_nxt)
            softmax_head(h, s_cur, mc_cur, p_cur)

    for h in range(N_HEADS):
        logits_head(h, k_ref[0, 0], bias_ref[0], s0_ref, mc0_ref)

    def pair_body(j, carry):
        c = 2 * j
        stage(c, s0_ref, mc0_ref, p0_ref, s1_ref, mc1_ref, p1_ref)

        @pl.when(c + 1 < nk)
        def _():
            stage(c + 1, s1_ref, mc1_ref, p1_ref, s0_ref, mc0_ref, p0_ref)

        return carry

    lax.fori_loop(0, (nk + 1) // 2, pair_body, 0)

    @pl.when((nk - 1) % 2 == 0)
    def _():
        pv_update(nk - 1, p0_ref)

    @pl.when((nk - 1) % 2 == 1)
    def _():
        pv_update(nk - 1, p1_ref)

    for h in range(N_HEADS):
        rows = slice(h * HEAD_DIM, (h + 1) * HEAD_DIM)
        acc_ref[rows, :] = acc_ref[rows, :] / l_ref[h, 0:1, :]
    yb_ref[0] = acc_ref[...].T.astype(BF16)


def _run_attn(qi, wi, ki, q, k, vt, k_sel):
    b, s, _ = q.shape
    nc = s // TK
    ki = ki.reshape(b, nc, TK, IDX_DIM)
    k = k.reshape(b, nc, TK, D_KV)
    rows = lambda width: pl.BlockSpec((1, TQ, width), lambda bi, i: (bi, i, 0))
    keys = lambda d0, d1: pl.BlockSpec((1, nc, d0, d1), lambda bi, i: (bi, 0, 0, 0))
    return pl.pallas_call(
        functools.partial(_attn_kernel, k_sel=k_sel),
        grid=(b, s // TQ),
        in_specs=[rows(IDX_HEADS * IDX_DIM), rows(LANES), keys(TK, IDX_DIM),
                  rows(D_ATTN), keys(TK, D_KV), keys(D_KV, TK)],
        out_specs=rows(D_ATTN),
        out_shape=jax.ShapeDtypeStruct((b, s, D_ATTN), BF16),
        scratch_shapes=[pltpu.VMEM((nc, TK, TQ), F32),
                        pltpu.VMEM((nc, TK, TQ), F32),
                        pltpu.VMEM((IDX_DIM, IDX_HEADS * TQ), BF16),
                        pltpu.VMEM((D_KV, N_HEADS * TQ), BF16),
                        pltpu.VMEM((TK, N_HEADS * TQ), F32),
                        pltpu.VMEM((TK, N_HEADS * TQ), F32),
                        pltpu.VMEM((TK, N_HEADS * TQ), BF16),
                        pltpu.VMEM((TK, N_HEADS * TQ), BF16),
                        pltpu.VMEM((N_HEADS, SUBLANES, TQ), F32),
                        pltpu.VMEM((N_HEADS, SUBLANES, TQ), F32),
                        pltpu.VMEM((N_HEADS, SUBLANES, TQ), F32),
                        pltpu.VMEM((N_HEADS, SUBLANES, TQ), F32),
                        pltpu.VMEM((N_HEADS, SUBLANES, TQ), F32),
                        pltpu.VMEM((D_ATTN, TQ), F32)],
        compiler_params=pltpu.CompilerParams(
            dimension_semantics=("parallel", "parallel"), vmem_limit_bytes=VMEM_LIMIT),
        name="dsa_attention",
    )(qi, wi, ki, q, k, vt)


def _tail_kernel(x_ref, ya_ref, yb_ref, wg_ref, wbr_ref, wo_ref, wup_ref, wdn_ref,
                 g1_ref, b1_ref, g2_ref, b2_ref, out_ref, *, alpha):
    half = TM_TAIL // 2
    halves = (slice(0, half), slice(half, TM_TAIL))

    def mix_pre_ln(rows):
        x = x_ref[rows, :]
        xb = x.astype(BF16)
        merged = jnp.zeros((half, D_MODEL), F32)
        for n, y_ref in enumerate((ya_ref, yb_ref)):
            gate = jnp.dot(xb, wg_ref[:, n * D_MODEL:(n + 1) * D_MODEL], preferred_element_type=F32)
            branch = jnp.dot(y_ref[rows, :], wbr_ref[n], preferred_element_type=F32)
            merged = merged + jax.nn.sigmoid(gate) * branch
        return alpha * x + jnp.dot(merged.astype(BF16), wo_ref[...], preferred_element_type=F32)

    def mlp_pre_ln(h):
        hb = h.astype(BF16)
        ff = jnp.zeros((half, D_MODEL), F32)
        for f in range(D_FF // FF_CHUNK):
            up = jnp.dot(hb, wup_ref[:, f * FF_CHUNK:(f + 1) * FF_CHUNK], preferred_element_type=F32)
            act = jnp.square(jnp.maximum(up, 0.0)).astype(BF16)
            ff = ff + jnp.dot(act, wdn_ref[f * FF_CHUNK:(f + 1) * FF_CHUNK, :], preferred_element_type=F32)
        return alpha * h + ff

    ln1 = lambda v: _layer_norm_rows(v, g1_ref[...], b1_ref[...])
    ln2 = lambda v: _layer_norm_rows(v, g2_ref[...], b2_ref[...])
    pre1_a = mix_pre_ln(halves[0])
    h_a = ln1(pre1_a)
    pre1_b = mix_pre_ln(halves[1])
    h_b = ln1(pre1_b)
    pre2_a = mlp_pre_ln(h_a)
    out_ref[halves[0], :] = ln2(pre2_a)
    pre2_b = mlp_pre_ln(h_b)
    out_ref[halves[1], :] = ln2(pre2_b)


def _run_tail(x2, ya2, yb2, wg, wbr, wo, wup, wdn, g1, b1, g2, b2, alpha):
    n, d = x2.shape
    const = lambda *shape: pl.BlockSpec(shape, lambda r: (0,) * len(shape),
                                        pipeline_mode=pl.Buffered(1))
    rows = lambda width: pl.BlockSpec((TM_TAIL, width), lambda r: (r, 0))
    return pl.pallas_call(
        functools.partial(_tail_kernel, alpha=alpha),
        grid=(n // TM_TAIL,),
        in_specs=[rows(d), rows(D_CONV), rows(D_ATTN),
                  const(d, N_BRANCH * d), const(N_BRANCH, D_CONV, d), const(d, d),
                  const(d, D_FF), const(D_FF, d),
                  const(1, d), const(1, d), const(1, d), const(1, d)],
        out_specs=rows(d),
        out_shape=jax.ShapeDtypeStruct((n, d), F32),
        compiler_params=pltpu.CompilerParams(
            dimension_semantics=("parallel",), vmem_limit_bytes=VMEM_LIMIT),
        name="merge_mlp",
    )(x2, ya2, yb2, wg, wbr, wo, wup, wdn, g1, b1, g2, b2)


def kernel(x, w_in, conv_w, idx_k_norm_g, idx_k_norm_b, w_branch, w_o, ln1_g, ln1_b, w_up, w_down, ln2_g, ln2_b):
    b, s, d = x.shape
    depth = w_in.shape[0]
    assert d == D_MODEL and s % TM_PROJ == 0 and (b * s) % TM_TAIL == 0 and TQ == TK
    k_sel = min(TOPK_MAX, s // 4)
    alpha = (2 * depth) ** 0.25
    for l in range(depth):
        wl = w_in[l]
        wa = wl[:, :OFF_KI].astype(BF16)
        wkw = jnp.concatenate(
            [wl[:, OFF_KI:OFF_WI], jnp.zeros((d, LANES - IDX_DIM), F32),
             wl[:, OFF_WI:OFF_GATES], jnp.zeros((d, LANES - IDX_HEADS), F32)], axis=1).astype(BF16)
        wg = wl[:, OFF_GATES:].astype(BF16)
        pad = lambda a: jnp.pad(a.astype(F32), (0, LANES - IDX_DIM)).reshape(1, LANES)
        ya, q, k, vt, qi, ki, wi = _run_proj(x, wa, wkw, conv_w[l], pad(idx_k_norm_g[l]), pad(idx_k_norm_b[l]))
        yb = _run_attn(qi, wi, ki, q, k, vt, k_sel)
        row = lambda a: a.astype(F32).reshape(1, d)
        out = _run_tail(x.reshape(b * s, d), ya.reshape(b * s, D_CONV), yb.reshape(b * s, D_ATTN),
                        wg, w_branch[l].astype(BF16), w_o[l].astype(BF16),
                        w_up[l].astype(BF16), w_down[l].astype(BF16),
                        row(ln1_g[l]), row(ln1_b[l]), row(ln2_g[l]), row(ln2_b[l]), alpha)
        x = out.reshape(b, s, d)
    return x
```

```python
import functools

import jax
import jax.numpy as jnp
from jax import lax
from jax.experimental import pallas as pl
from jax.experimental.pallas import tpu as pltpu

D_MODEL = 1024
D_CONV = 512
CONV_WIDTH = 3
N_HEADS = 8
N_KV_HEADS = 2
HEAD_DIM = 64
D_ATTN = N_HEADS * HEAD_DIM
D_KV = N_KV_HEADS * HEAD_DIM
IDX_HEADS = 8
IDX_DIM = 64
TOPK_MAX = 256
N_BRANCH = 2
D_FF = 4 * D_MODEL
LN_EPS = 1e-5
IDX_SCALE = (IDX_DIM ** -0.5) * (IDX_HEADS ** -0.5)
ATTN_SCALE = HEAD_DIM ** -0.5
LOG2_E = 1.4426950408889634
Q_SCALE = ATTN_SCALE * LOG2_E

OFF_Q = 3 * D_CONV
OFF_K = OFF_Q + D_ATTN
OFF_V = OFF_K + D_KV
OFF_QI = OFF_V + D_KV
OFF_KI = OFF_QI + IDX_HEADS * IDX_DIM
OFF_WI = OFF_KI + IDX_DIM
OFF_GATES = OFF_WI + IDX_HEADS

LANES = 128
SUBLANES = 8
BF16_SUBLANES = 16
VMEM_LIMIT = 56 * 1024 * 1024

TM_PROJ = 512
TQ = 256
TK = 256
TM_TAIL = 512
FF_CHUNK = 1024
COUNT_CHAINS = 4
EXP_ROWS = 32

F32 = jnp.float32
BF16 = jnp.bfloat16
NEG_INF = float("-inf")
F32_MAX = float(jnp.finfo(jnp.float32).max)
INT_MIN = -2 ** 31


def _dot_nt(a, b_t):
    return lax.dot_general(a, b_t, (((1,), (1,)), ((), ())), preferred_element_type=F32)


def _layer_norm_rows(x, g, b):
    mu = jnp.mean(x, axis=-1, keepdims=True)
    xc = x - mu
    var = jnp.mean(xc * xc, axis=-1, keepdims=True)
    return xc * lax.rsqrt(var + LN_EPS) * g + b


def _proj_kernel(x_ref, wa_ref, wkw_ref, convw_ref, kg_ref, kb_ref,
                 ya_ref, q_ref, k_ref, vt_ref, qi_ref, ki_ref, wi_ref,
                 carry_ref):
    @pl.when(pl.program_id(1) == 0)
    def _():
        carry_ref[...] = jnp.zeros_like(carry_ref)

    xb = x_ref[0].astype(BF16)

    def proj(lo, hi):
        return _dot_nt(xb, wa_ref[lo:hi, :])

    zkw = _dot_nt(xb, wkw_ref[...])
    wi_ref[0] = zkw[:, LANES:]
    zk = zkw[:, :LANES]
    valid = lax.broadcasted_iota(jnp.int32, zk.shape, 1) < IDX_DIM
    mu = jnp.sum(jnp.where(valid, zk, 0.0), axis=-1, keepdims=True) * (1.0 / IDX_DIM)
    xc = jnp.where(valid, zk - mu, 0.0)
    var = jnp.sum(xc * xc, axis=-1, keepdims=True) * (1.0 / IDX_DIM)
    kin = xc * lax.rsqrt(var + LN_EPS) * kg_ref[...] + kb_ref[...]
    ki_ref[0] = kin[:, :IDX_DIM].astype(BF16)

    vt = proj(OFF_V, OFF_QI).T
    for c in range(TM_PROJ // TK):
        vt_ref[0, c] = vt[:, c * TK:(c + 1) * TK].astype(BF16)

    cu = proj(D_CONV, 2 * D_CONV) * proj(2 * D_CONV, 3 * D_CONV)
    prev = carry_ref[...]
    row = lax.broadcasted_iota(jnp.int32, cu.shape, 0)
    cu1 = jnp.where(row == 0, prev[SUBLANES - 1:SUBLANES, :], pltpu.roll(cu, 1, axis=0))
    cu2 = jnp.where(row == 0, prev[SUBLANES - 2:SUBLANES - 1, :],
                    jnp.where(row == 1, prev[SUBLANES - 1:SUBLANES, :], pltpu.roll(cu, 2, axis=0)))
    carry_ref[...] = cu[TM_PROJ - SUBLANES:, :]
    cw = convw_ref[...]
    conv = cw[0:1, :] * cu2 + cw[1:2, :] * cu1 + cw[2:3, :] * cu
    ya_ref[0] = (proj(0, D_CONV) * conv).astype(BF16)

    k_ref[0] = proj(OFF_K, OFF_V).astype(BF16)
    q_ref[0] = (proj(OFF_Q, OFF_K) * Q_SCALE).astype(BF16)
    qi_ref[0] = proj(OFF_QI, OFF_KI).astype(BF16)


def _run_proj(x, wa, wkw, conv_w, kg, kb):
    b, s, d = x.shape
    nc = s // TK
    cpt = TM_PROJ // TK
    const = lambda *shape: pl.BlockSpec(shape, lambda bi, j: (0,) * len(shape),
                                        pipeline_mode=pl.Buffered(1))
    rows = lambda width: pl.BlockSpec((1, TM_PROJ, width), lambda bi, j: (bi, j, 0))
    return pl.pallas_call(
        _proj_kernel,
        grid=(b, s // TM_PROJ),
        in_specs=[rows(d), const(OFF_KI, d), const(2 * LANES, d), const(CONV_WIDTH, D_CONV),
                  const(1, LANES), const(1, LANES)],
        out_specs=[rows(D_CONV), rows(D_ATTN), rows(D_KV),
                   pl.BlockSpec((1, cpt, D_KV, TK), lambda bi, j: (bi, j, 0, 0)),
                   rows(IDX_HEADS * IDX_DIM), rows(IDX_DIM), rows(LANES)],
        out_shape=[jax.ShapeDtypeStruct((b, s, D_CONV), BF16),
                   jax.ShapeDtypeStruct((b, s, D_ATTN), BF16),
                   jax.ShapeDtypeStruct((b, s, D_KV), BF16),
                   jax.ShapeDtypeStruct((b, nc, D_KV, TK), BF16),
                   jax.ShapeDtypeStruct((b, s, IDX_HEADS * IDX_DIM), BF16),
                   jax.ShapeDtypeStruct((b, s, IDX_DIM), BF16),
                   jax.ShapeDtypeStruct((b, s, LANES), F32)],
        scratch_shapes=[pltpu.VMEM((SUBLANES, D_CONV), F32)],
        compiler_params=pltpu.CompilerParams(
            dimension_semantics=("parallel", "arbitrary"), vmem_limit_bytes=VMEM_LIMIT),
        name="proj_mixer_a",
    )(x, wa, wkw, conv_w, kg, kb)


def _sortable_to_f32(x):
    bits = x ^ ((x >> 31) & jnp.int32(0x7FFFFFFF))
    return pltpu.bitcast(bits, F32)


def _attn_kernel(qi_ref, wi_ref, ki_ref, q_ref, k_ref, vt_ref, yb_ref,
                 score_ref, bias_ref, qit_ref, qpad_ref, s0_ref, s1_ref, p0_ref, p1_ref,
                 mc0_ref, mc1_ref, alpha_ref, m_ref, l_ref, acc_ref, *, k_sel):
    i = pl.program_id(1)
    nk = i + 1
    kf = float(k_sel)
    rep_heads = N_HEADS // N_KV_HEADS
    n_groups = TK // SUBLANES

    qit = qi_ref[0].astype(F32).T.astype(BF16)
    for h in range(IDX_HEADS):
        qit_ref[:, h * TQ:(h + 1) * TQ] = qit[h * IDX_DIM:(h + 1) * IDX_DIM, :]
    wt = wi_ref[0].T[:IDX_HEADS, :]
    qt = q_ref[0].astype(F32).T.astype(BF16)
    zero_half = jnp.zeros((HEAD_DIM, TQ), BF16)
    for h in range(N_HEADS):
        g = h // rep_heads
        for gg in range(N_KV_HEADS):
            qpad_ref[gg * HEAD_DIM:(gg + 1) * HEAD_DIM, h * TQ:(h + 1) * TQ] = (
                qt[h * HEAD_DIM:(h + 1) * HEAD_DIM, :] if gg == g else zero_half)

    row = lax.broadcasted_iota(jnp.int32, (TK, TQ), 0)
    col = lax.broadcasted_iota(jnp.int32, (TK, TQ), 1)
    row_minus_col = row - col

    def score_body(c, carry):
        kic = ki_ref[0, c]
        lg_all = jnp.dot(kic, qit_ref[...], preferred_element_type=F32)
        acc = jnp.zeros((TK, TQ), F32)
        for h in range(IDX_HEADS):
            lg = lg_all[:, h * TQ:(h + 1) * TQ]
            acc = acc + wt[h:h + 1, :] * jnp.maximum(lg, 0.0)
        noncausal = row_minus_col > (i - c) * TQ
        score_ref[c] = jnp.where(noncausal, NEG_INF, acc * IDX_SCALE)
        return carry

    lax.fori_loop(0, nk, score_body, 0)

    def count_where(pred):
        def body(c, accs):
            accs = list(accs)
            for r in range(n_groups):
                blk = score_ref[c, r * SUBLANES:(r + 1) * SUBLANES, :]
                a = accs[r % COUNT_CHAINS]
                accs[r % COUNT_CHAINS] = jnp.where(pred(blk), a + 1.0, a)
            return tuple(accs)
        accs = lax.fori_loop(0, nk, body, (jnp.zeros((SUBLANES, TQ), F32),) * COUNT_CHAINS)
        return jnp.sum(sum(accs[1:], accs[0]), axis=0, keepdims=True)

    def count_ge(cand_key):
        cand = jnp.broadcast_to(_sortable_to_f32(cand_key), (SUBLANES, TQ))
        return count_where(lambda blk: blk >= cand)

    cnt0 = count_ge(jnp.zeros((1, TQ), jnp.int32))
    ok0 = cnt0 >= kf
    prefix0 = jnp.where(ok0, jnp.int32(0), jnp.int32(INT_MIN))
    cntp0 = jnp.where(ok0, cnt0, 0.0)

    def bit_body(b, carry):
        prefix, cntp = carry
        cand_key = prefix | jnp.left_shift(jnp.int32(1), 30 - b)
        cnt = count_ge(cand_key)
        ok = cnt >= kf
        return jnp.where(ok, cand_key, prefix), jnp.where(ok, cnt, cntp)

    prefix, cntp = lax.fori_loop(0, 31, bit_body, (prefix0, cntp0))
    tau = jnp.where(prefix == INT_MIN, -F32_MAX, _sortable_to_f32(prefix))

    def bias_body(c, carry):
        bias_ref[c] = jnp.where(score_ref[c] >= tau, 0.0, NEG_INF)
        return carry

    lax.fori_loop(0, nk, bias_body, 0)

    @pl.when(jnp.max(cntp) > kf)
    def _():
        tau8 = jnp.broadcast_to(tau, (SUBLANES, TQ))
        need = kf - count_where(lambda blk: blk > tau8)
        r = lax.broadcasted_iota(jnp.int32, (TK, TK), 0)
        cc = lax.broadcasted_iota(jnp.int32, (TK, TK), 1)
        before = jnp.where(cc < r, 1.0, 0.0).astype(BF16)

        def tie_body(c, seen):
            sc = score_ref[c]
            eq = sc == tau
            eqf = jnp.where(eq, 1.0, 0.0)
            rank = jnp.dot(before, eqf.astype(BF16), preferred_element_type=F32) + seen
            keep = (sc > tau) | (eq & (rank < need))
            bias_ref[c] = jnp.where(keep, 0.0, NEG_INF)
            return seen + jnp.sum(eqf, axis=0, keepdims=True)

        lax.fori_loop(0, nk, tie_body, jnp.zeros((1, TQ), F32))

    m_ref[...] = jnp.full(m_ref.shape, NEG_INF, F32)
    l_ref[...] = jnp.zeros(l_ref.shape, F32)
    acc_ref[...] = jnp.zeros(acc_ref.shape, F32)
    alpha_ref[...] = jnp.ones(alpha_ref.shape, F32)
    p1_ref[...] = jnp.zeros(p1_ref.shape, BF16)
    ones_rows = jnp.ones((BF16_SUBLANES, TK), BF16)

    def logits_head(h, kc, bias, s_out, mc_out):
        cols = slice(h * TQ, (h + 1) * TQ)
        s = jnp.dot(kc, qpad_ref[:, cols], preferred_element_type=F32) + bias
        s_out[:, cols] = s
        mc_out[h] = jnp.broadcast_to(jnp.max(s, axis=0, keepdims=True), (SUBLANES, TQ))

    def pv_update(c, p_in):
        for g in range(N_KV_HEADS):
            lanes = slice(g * rep_heads * TQ, (g + 1) * rep_heads * TQ)
            lhs = jnp.concatenate([vt_ref[0, c, g * HEAD_DIM:(g + 1) * HEAD_DIM, :], ones_rows], axis=0)
            pv = jnp.dot(lhs, p_in[:, lanes], preferred_element_type=F32)
            for hh in range(rep_heads):
                h = g * rep_heads + hh
                rows = slice(h * HEAD_DIM, (h + 1) * HEAD_DIM)
                cols = slice(hh * TQ, (hh + 1) * TQ)
                alpha = alpha_ref[h, 0:1, :]
                acc_ref[rows, :] = alpha * acc_ref[rows, :] + pv[:HEAD_DIM, cols]
                l_ref[h] = jnp.broadcast_to(alpha * l_ref[h, 0:1, :] + pv[HEAD_DIM:HEAD_DIM + 1, cols],
                                            (SUBLANES, TQ))

    def softmax_head(h, s_in, mc_in, p_out):
        m_old = m_ref[h, 0:1, :]
        m_new = jnp.maximum(m_old, mc_in[h, 0:1, :])
        m_use = jnp.where(m_new == NEG_INF, 0.0, m_new)
        alpha_ref[h] = jnp.broadcast_to(jnp.exp2(m_old - m_use), (SUBLANES, TQ))
        m_ref[h] = jnp.broadcast_to(m_new, (SUBLANES, TQ))
        cols = slice(h * TQ, (h + 1) * TQ)
        for blk in range(TK // EXP_ROWS):
            rows = slice(blk * EXP_ROWS, (blk + 1) * EXP_ROWS)
            p_out[rows, cols] = jnp.exp2(s_in[rows, cols] - m_use).astype(BF16)

    def stage(c, s_cur, mc_cur, p_cur, s_nxt, mc_nxt, p_prev):
        c_nxt = jnp.minimum(c + 1, nk - 1)
        pv_update(jnp.maximum(c - 1, 0), p_prev)
        bias_nxt = bias_ref[c_nxt]
        k_nxt = k_ref[0, c_nxt]
        for h in range(N_HEADS):
            logits_head(h, k_nxt, bias_nxt, s_nxt, mc_nxt)
            softmax_head(h, s_cur, mc_cur, p_cur)

    for h in range(N_HEADS):
        logits_head(h, k_ref[0, 0], bias_ref[0], s0_ref, mc0_ref)

    def pair_body(j, carry):
        c = 2 * j
        stage(c, s0_ref, mc0_ref, p0_ref, s1_ref, mc1_ref, p1_ref)

        @pl.when(c + 1 < nk)
        def _():
            stage(c + 1, s1_ref, mc1_ref, p1_ref, s0_ref, mc0_ref, p0_ref)

        return carry

    lax.fori_loop(0, (nk + 1) // 2, pair_body, 0)

    @pl.when((nk - 1) % 2 == 0)
    def _():
        pv_update(nk - 1, p0_ref)

    @pl.when((nk - 1) % 2 == 1)
    def _():
        pv_update(nk - 1, p1_ref)

    for h in range(N_HEADS):
        rows = slice(h * HEAD_DIM, (h + 1) * HEAD_DIM)
        acc_ref[rows, :] = acc_ref[rows, :] / l_ref[h, 0:1, :]
    yb_ref[0] = acc_ref[...].T.astype(BF16)


def _run_attn(qi, wi, ki, q, k, vt, k_sel):
    b, s, _ = q.shape
    nc = s // TK
    ki = ki.reshape(b, nc, TK, IDX_DIM)
    k = k.reshape(b, nc, TK, D_KV)
    rows = lambda width: pl.BlockSpec((1, TQ, width), lambda bi, i: (bi, i, 0))
    keys = lambda d0, d1: pl.BlockSpec((1, nc, d0, d1), lambda bi, i: (bi, 0, 0, 0))
    return pl.pallas_call(
        functools.partial(_attn_kernel, k_sel=k_sel),
        grid=(b, s // TQ),
        in_specs=[rows(IDX_HEADS * IDX_DIM), rows(LANES), keys(TK, IDX_DIM),
                  rows(D_ATTN), keys(TK, D_KV), keys(D_KV, TK)],
        out_specs=rows(D_ATTN),
        out_shape=jax.ShapeDtypeStruct((b, s, D_ATTN), BF16),
        scratch_shapes=[pltpu.VMEM((nc, TK, TQ), F32),
                        pltpu.VMEM((nc, TK, TQ), F32),
                        pltpu.VMEM((IDX_DIM, IDX_HEADS * TQ), BF16),
                        pltpu.VMEM((D_KV, N_HEADS * TQ), BF16),
                        pltpu.VMEM((TK, N_HEADS * TQ), F32),
                        pltpu.VMEM((TK, N_HEADS * TQ), F32),
                        pltpu.VMEM((TK, N_HEADS * TQ), BF16),
                        pltpu.VMEM((TK, N_HEADS * TQ), BF16),
                        pltpu.VMEM((N_HEADS, SUBLANES, TQ), F32),
                        pltpu.VMEM((N_HEADS, SUBLANES, TQ), F32),
                        pltpu.VMEM((N_HEADS, SUBLANES, TQ), F32),
                        pltpu.VMEM((N_HEADS, SUBLANES, TQ), F32),
                        pltpu.VMEM((N_HEADS, SUBLANES, TQ), F32),
                        pltpu.VMEM((D_ATTN, TQ), F32)],
        compiler_params=pltpu.CompilerParams(
            dimension_semantics=("parallel", "parallel"), vmem_limit_bytes=VMEM_LIMIT),
        name="dsa_attention",
    )(qi, wi, ki, q, k, vt)


def _tail_kernel(x_ref, ya_ref, yb_ref, wg_ref, wbr_ref, wo_ref, wup_ref, wdn_ref,
                 g1_ref, b1_ref, g2_ref, b2_ref, out_ref, *, alpha):
    half = TM_TAIL // 2
    halves = (slice(0, half), slice(half, TM_TAIL))

    def mix_pre_ln(rows):
        x = x_ref[rows, :]
        xb = x.astype(BF16)
        merged = jnp.zeros((half, D_MODEL), F32)
        for n, y_ref in enumerate((ya_ref, yb_ref)):
            gate = _dot_nt(xb, wg_ref[n * D_MODEL:(n + 1) * D_MODEL, :])
            branch = jnp.dot(y_ref[rows, :], wbr_ref[n], preferred_element_type=F32)
            merged = merged + jax.nn.sigmoid(gate) * branch
        return alpha * x + jnp.dot(merged.astype(BF16), wo_ref[...], preferred_element_type=F32)

    def mlp_pre_ln(h):
        hb = h.astype(BF16)
        ff = jnp.zeros((half, D_MODEL), F32)
        for f in range(D_FF // FF_CHUNK):
            up = jnp.dot(hb, wup_ref[:, f * FF_CHUNK:(f + 1) * FF_CHUNK], preferred_element_type=F32)
            act = jnp.square(jnp.maximum(up, 0.0)).astype(BF16)
            ff = ff + jnp.dot(act, wdn_ref[f * FF_CHUNK:(f + 1) * FF_CHUNK, :], preferred_element_type=F32)
        return alpha * h + ff

    ln1 = lambda v: _layer_norm_rows(v, g1_ref[...], b1_ref[...])
    ln2 = lambda v: _layer_norm_rows(v, g2_ref[...], b2_ref[...])
    pre1_a = mix_pre_ln(halves[0])
    h_a = ln1(pre1_a)
    pre1_b = mix_pre_ln(halves[1])
    h_b = ln1(pre1_b)
    pre2_a = mlp_pre_ln(h_a)
    out_ref[halves[0], :] = ln2(pre2_a)
    pre2_b = mlp_pre_ln(h_b)
    out_ref[halves[1], :] = ln2(pre2_b)


def _run_tail(x2, ya2, yb2, wg, wbr, wo, wup, wdn, g1, b1, g2, b2, alpha):
    n, d = x2.shape
    const = lambda *shape: pl.BlockSpec(shape, lambda r: (0,) * len(shape),
                                        pipeline_mode=pl.Buffered(1))
    rows = lambda width: pl.BlockSpec((TM_TAIL, width), lambda r: (r, 0))
    return pl.pallas_call(
        functools.partial(_tail_kernel, alpha=alpha),
        grid=(n // TM_TAIL,),
        in_specs=[rows(d), rows(D_CONV), rows(D_ATTN),
                  const(N_BRANCH * d, d), const(N_BRANCH, D_CONV, d), const(d, d),
                  const(d, D_FF), const(D_FF, d),
                  const(1, d), const(1, d), const(1, d), const(1, d)],
        out_specs=rows(d),
        out_shape=jax.ShapeDtypeStruct((n, d), F32),
        compiler_params=pltpu.CompilerParams(
            dimension_semantics=("parallel",), vmem_limit_bytes=VMEM_LIMIT),
        name="merge_mlp",
    )(x2, ya2, yb2, wg, wbr, wo, wup, wdn, g1, b1, g2, b2)


def kernel(x, w_in, conv_w, idx_k_norm_g, idx_k_norm_b, w_branch, w_o, ln1_g, ln1_b, w_up, w_down, ln2_g, ln2_b):
    b, s, d = x.shape
    depth = w_in.shape[0]
    assert d == D_MODEL and s % TM_PROJ == 0 and (b * s) % TM_TAIL == 0 and TQ == TK
    k_sel = min(TOPK_MAX, s // 4)
    alpha = (2 * depth) ** 0.25
    for l in range(depth):
        wt = jnp.swapaxes(w_in[l], 0, 1)
        wa = wt[:OFF_KI].astype(BF16)
        wkw = jnp.concatenate(
            [wt[OFF_KI:OFF_WI], jnp.zeros((LANES - IDX_DIM, d), F32),
             wt[OFF_WI:OFF_GATES], jnp.zeros((LANES - IDX_HEADS, d), F32)], axis=0).astype(BF16)
        wg = wt[OFF_GATES:].astype(BF16)
        pad = lambda a: jnp.pad(a.astype(F32), (0, LANES - IDX_DIM)).reshape(1, LANES)
        ya, q, k, vt, qi, ki, wi = _run_proj(x, wa, wkw, conv_w[l], pad(idx_k_norm_g[l]), pad(idx_k_norm_b[l]))
        yb = _run_attn(qi, wi, ki, q, k, vt, k_sel)
        row = lambda a: a.astype(F32).reshape(1, d)
        out = _run_tail(x.reshape(b * s, d), ya.reshape(b * s, D_CONV), yb.reshape(b * s, D_ATTN),
                        wg, w_branch[l].astype(BF16), w_o[l].astype(BF16),
                        w_up[l].astype(BF16), w_down[l].astype(BF16),
                        row(ln1_g[l]), row(ln1_b[l]), row(ln2_g[l]), row(ln2_b[l]), alpha)
        x = out.reshape(b, s, d)
    return x
```

```python
import functools

import jax
import jax.numpy as jnp
from jax import lax
from jax.experimental import pallas as pl
from jax.experimental.pallas import tpu as pltpu

D_MODEL = 1024
D_CONV = 512
CONV_WIDTH = 3
N_HEADS = 8
N_KV_HEADS = 2
HEAD_DIM = 64
D_ATTN = N_HEADS * HEAD_DIM
D_KV = N_KV_HEADS * HEAD_DIM
IDX_HEADS = 8
IDX_DIM = 64
TOPK_MAX = 256
N_BRANCH = 2
D_FF = 4 * D_MODEL
LN_EPS = 1e-5
IDX_SCALE = (IDX_DIM ** -0.5) * (IDX_HEADS ** -0.5)
ATTN_SCALE = HEAD_DIM ** -0.5
LOG2_E = 1.4426950408889634
Q_SCALE = ATTN_SCALE * LOG2_E

OFF_Q = 3 * D_CONV
OFF_K = OFF_Q + D_ATTN
OFF_V = OFF_K + D_KV
OFF_QI = OFF_V + D_KV
OFF_KI = OFF_QI + IDX_HEADS * IDX_DIM
OFF_WI = OFF_KI + IDX_DIM
OFF_GATES = OFF_WI + IDX_HEADS

LANES = 128
SUBLANES = 8
BF16_SUBLANES = 16
VMEM_LIMIT = 56 * 1024 * 1024

TM_PROJ = 1024
TQ = 256
TK = 256
TM_TAIL = 512
FF_CHUNK = 1024
COUNT_CHAINS = 4
EXP_ROWS = 32

F32 = jnp.float32
BF16 = jnp.bfloat16
NEG_INF = float("-inf")
F32_MAX = float(jnp.finfo(jnp.float32).max)
INT_MIN = -2 ** 31


def _dot_nt(a, b_t):
    return lax.dot_general(a, b_t, (((1,), (1,)), ((), ())), preferred_element_type=F32)


def _layer_norm_rows(x, g, b):
    mu = jnp.mean(x, axis=-1, keepdims=True)
    xc = x - mu
    var = jnp.mean(xc * xc, axis=-1, keepdims=True)
    return xc * lax.rsqrt(var + LN_EPS) * g + b


def _proj_kernel(x_ref, wa_ref, wkw_ref, convw_ref, kg_ref, kb_ref,
                 ya_ref, q_ref, k_ref, vt_ref, qi_ref, ki_ref, wi_ref,
                 carry_ref):
    @pl.when(pl.program_id(1) == 0)
    def _():
        carry_ref[...] = jnp.zeros_like(carry_ref)

    xb = x_ref[0].astype(BF16)

    def proj(lo, hi):
        return _dot_nt(xb, wa_ref[lo:hi, :])

    zkw = _dot_nt(xb, wkw_ref[...])
    wi_ref[0] = zkw[:, LANES:]
    zk = zkw[:, :LANES]
    valid = lax.broadcasted_iota(jnp.int32, zk.shape, 1) < IDX_DIM
    mu = jnp.sum(jnp.where(valid, zk, 0.0), axis=-1, keepdims=True) * (1.0 / IDX_DIM)
    xc = jnp.where(valid, zk - mu, 0.0)
    var = jnp.sum(xc * xc, axis=-1, keepdims=True) * (1.0 / IDX_DIM)
    kin = xc * lax.rsqrt(var + LN_EPS) * kg_ref[...] + kb_ref[...]
    ki_ref[0] = kin[:, :IDX_DIM].astype(BF16)

    vt = proj(OFF_V, OFF_QI).T
    for c in range(TM_PROJ // TK):
        vt_ref[0, c] = vt[:, c * TK:(c + 1) * TK].astype(BF16)

    cu = proj(D_CONV, 2 * D_CONV) * proj(2 * D_CONV, 3 * D_CONV)
    prev = carry_ref[...]
    row = lax.broadcasted_iota(jnp.int32, cu.shape, 0)
    cu1 = jnp.where(row == 0, prev[SUBLANES - 1:SUBLANES, :], pltpu.roll(cu, 1, axis=0))
    cu2 = jnp.where(row == 0, prev[SUBLANES - 2:SUBLANES - 1, :],
                    jnp.where(row == 1, prev[SUBLANES - 1:SUBLANES, :], pltpu.roll(cu, 2, axis=0)))
    carry_ref[...] = cu[TM_PROJ - SUBLANES:, :]
    cw = convw_ref[...]
    conv = cw[0:1, :] * cu2 + cw[1:2, :] * cu1 + cw[2:3, :] * cu
    ya_ref[0] = (proj(0, D_CONV) * conv).astype(BF16)

    k_ref[0] = proj(OFF_K, OFF_V).astype(BF16)
    q_ref[0] = (proj(OFF_Q, OFF_K) * Q_SCALE).astype(BF16)
    qi_ref[0] = proj(OFF_QI, OFF_KI).astype(BF16)


def _run_proj(x, wa, wkw, conv_w, kg, kb):
    b, s, d = x.shape
    nc = s // TK
    cpt = TM_PROJ // TK
    const = lambda *shape: pl.BlockSpec(shape, lambda bi, j: (0,) * len(shape),
                                        pipeline_mode=pl.Buffered(1))
    rows = lambda width: pl.BlockSpec((1, TM_PROJ, width), lambda bi, j: (bi, j, 0))
    return pl.pallas_call(
        _proj_kernel,
        grid=(b, s // TM_PROJ),
        in_specs=[rows(d), const(OFF_KI, d), const(2 * LANES, d), const(CONV_WIDTH, D_CONV),
                  const(1, LANES), const(1, LANES)],
        out_specs=[rows(D_CONV), rows(D_ATTN), rows(D_KV),
                   pl.BlockSpec((1, cpt, D_KV, TK), lambda bi, j: (bi, j, 0, 0)),
                   rows(IDX_HEADS * IDX_DIM), rows(IDX_DIM), rows(LANES)],
        out_shape=[jax.ShapeDtypeStruct((b, s, D_CONV), BF16),
                   jax.ShapeDtypeStruct((b, s, D_ATTN), BF16),
                   jax.ShapeDtypeStruct((b, s, D_KV), BF16),
                   jax.ShapeDtypeStruct((b, nc, D_KV, TK), BF16),
                   jax.ShapeDtypeStruct((b, s, IDX_HEADS * IDX_DIM), BF16),
                   jax.ShapeDtypeStruct((b, s, IDX_DIM), BF16),
                   jax.ShapeDtypeStruct((b, s, LANES), F32)],
        scratch_shapes=[pltpu.VMEM((SUBLANES, D_CONV), F32)],
        compiler_params=pltpu.CompilerParams(
            dimension_semantics=("parallel", "arbitrary"), vmem_limit_bytes=VMEM_LIMIT),
        name="proj_mixer_a",
    )(x, wa, wkw, conv_w, kg, kb)


def _sortable_to_f32(x):
    bits = x ^ ((x >> 31) & jnp.int32(0x7FFFFFFF))
    return pltpu.bitcast(bits, F32)


def _attn_kernel(qi_ref, wi_ref, ki_ref, q_ref, k_ref, vt_ref, yb_ref,
                 score_ref, bias_ref, qit_ref, qpad_ref, s0_ref, s1_ref, p0_ref, p1_ref,
                 mc0_ref, mc1_ref, alpha_ref, m_ref, l_ref, acc_ref, thr_ref, *, k_sel):
    i = pl.program_id(1)
    nk = i + 1
    kf = float(k_sel)
    rep_heads = N_HEADS // N_KV_HEADS
    n_groups = TK // SUBLANES
    n_lane_groups = TQ // LANES

    qit = qi_ref[0].astype(F32).T.astype(BF16)
    for h in range(IDX_HEADS):
        qit_ref[:, h * TQ:(h + 1) * TQ] = qit[h * IDX_DIM:(h + 1) * IDX_DIM, :]
    wt = wi_ref[0].T[:IDX_HEADS, :]
    qt = q_ref[0].astype(F32).T.astype(BF16)
    zero_half = jnp.zeros((HEAD_DIM, TQ), BF16)
    for h in range(N_HEADS):
        g = h // rep_heads
        for gg in range(N_KV_HEADS):
            qpad_ref[gg * HEAD_DIM:(gg + 1) * HEAD_DIM, h * TQ:(h + 1) * TQ] = (
                qt[h * HEAD_DIM:(h + 1) * HEAD_DIM, :] if gg == g else zero_half)

    row = lax.broadcasted_iota(jnp.int32, (TK, TQ), 0)
    col = lax.broadcasted_iota(jnp.int32, (TK, TQ), 1)
    row_minus_col = row - col

    def score_body(c, carry):
        kic = ki_ref[0, c]
        lg_all = jnp.dot(kic, qit_ref[...], preferred_element_type=F32)
        acc = jnp.zeros((TK, TQ), F32)
        for h in range(IDX_HEADS):
            lg = lg_all[:, h * TQ:(h + 1) * TQ]
            acc = acc + wt[h:h + 1, :] * jnp.maximum(lg, 0.0)
        noncausal = row_minus_col > (i - c) * TQ
        score_ref[c] = jnp.where(noncausal, NEG_INF, acc * IDX_SCALE)
        return carry

    lax.fori_loop(0, nk, score_body, 0)

    def count_where(pred, thr):
        thr_ref[...] = jnp.broadcast_to(thr, (SUBLANES, TQ))
        thr_g = [thr_ref[:, g * LANES:(g + 1) * LANES] for g in range(n_lane_groups)]

        def chunk_counts(c, accs, diagonal):
            accs = [list(a) for a in accs]
            for r in range(n_groups):
                for g in range(n_lane_groups):
                    if diagonal and r * SUBLANES >= (g + 1) * LANES:
                        continue
                    blk = score_ref[c, r * SUBLANES:(r + 1) * SUBLANES, g * LANES:(g + 1) * LANES]
                    a = accs[g][r % COUNT_CHAINS]
                    accs[g][r % COUNT_CHAINS] = jnp.where(pred(blk, thr_g[g]), a + 1.0, a)
            return tuple(tuple(a) for a in accs)

        zero = jnp.zeros((SUBLANES, LANES), F32)
        accs = ((zero,) * COUNT_CHAINS,) * n_lane_groups
        accs = lax.fori_loop(0, nk - 1, lambda c, a: chunk_counts(c, a, False), accs)
        accs = chunk_counts(nk - 1, accs, True)
        per_group = [sum(a[1:], a[0]) for a in accs]
        return jnp.sum(jnp.concatenate(per_group, axis=1), axis=0, keepdims=True)

    def count_ge(cand_key):
        return count_where(lambda blk, thr: blk >= thr, _sortable_to_f32(cand_key))

    cnt0 = count_ge(jnp.zeros((1, TQ), jnp.int32))
    ok0 = cnt0 >= kf
    prefix0 = jnp.where(ok0, jnp.int32(0), jnp.int32(INT_MIN))
    cntp0 = jnp.where(ok0, cnt0, 0.0)

    def bit_body(b, carry):
        prefix, cntp = carry
        cand_key = prefix | jnp.left_shift(jnp.int32(1), 30 - b)
        cnt = count_ge(cand_key)
        ok = cnt >= kf
        return jnp.where(ok, cand_key, prefix), jnp.where(ok, cnt, cntp)

    prefix, cntp = lax.fori_loop(0, 31, bit_body, (prefix0, cntp0))
    tau = jnp.where(prefix == INT_MIN, -F32_MAX, _sortable_to_f32(prefix))

    def bias_body(c, carry):
        bias_ref[c] = jnp.where(score_ref[c] >= tau, 0.0, NEG_INF)
        return carry

    lax.fori_loop(0, nk, bias_body, 0)

    @pl.when(jnp.max(cntp) > kf)
    def _():
        need = kf - count_where(lambda blk, thr: blk > thr, tau)
        r = lax.broadcasted_iota(jnp.int32, (TK, TK), 0)
        cc = lax.broadcasted_iota(jnp.int32, (TK, TK), 1)
        before = jnp.where(cc < r, 1.0, 0.0).astype(BF16)

        def tie_body(c, seen):
            sc = score_ref[c]
            eq = sc == tau
            eqf = jnp.where(eq, 1.0, 0.0)
            rank = jnp.dot(before, eqf.astype(BF16), preferred_element_type=F32) + seen
            keep = (sc > tau) | (eq & (rank < need))
            bias_ref[c] = jnp.where(keep, 0.0, NEG_INF)
            return seen + jnp.sum(eqf, axis=0, keepdims=True)

        lax.fori_loop(0, nk, tie_body, jnp.zeros((1, TQ), F32))

    m_ref[...] = jnp.full(m_ref.shape, NEG_INF, F32)
    l_ref[...] = jnp.zeros(l_ref.shape, F32)
    acc_ref[...] = jnp.zeros(acc_ref.shape, F32)
    alpha_ref[...] = jnp.ones(alpha_ref.shape, F32)
    p1_ref[...] = jnp.zeros(p1_ref.shape, BF16)
    ones_rows = jnp.ones((BF16_SUBLANES, TK), BF16)

    def logits_head(h, kc, bias, s_out, mc_out):
        cols = slice(h * TQ, (h + 1) * TQ)
        s = jnp.dot(kc, qpad_ref[:, cols], preferred_element_type=F32) + bias
        s_out[:, cols] = s
        mc_out[h] = jnp.broadcast_to(jnp.max(s, axis=0, keepdims=True), (SUBLANES, TQ))

    def pv_update(c, p_in):
        for g in range(N_KV_HEADS):
            lanes = slice(g * rep_heads * TQ, (g + 1) * rep_heads * TQ)
            lhs = jnp.concatenate([vt_ref[0, c, g * HEAD_DIM:(g + 1) * HEAD_DIM, :], ones_rows], axis=0)
            pv = jnp.dot(lhs, p_in[:, lanes], preferred_element_type=F32)
            for hh in range(rep_heads):
                h = g * rep_heads + hh
                rows = slice(h * HEAD_DIM, (h + 1) * HEAD_DIM)
                cols = slice(hh * TQ, (hh + 1) * TQ)
                alpha = alpha_ref[h, 0:1, :]
                acc_ref[rows, :] = alpha * acc_ref[rows, :] + pv[:HEAD_DIM, cols]
                l_ref[h] = jnp.broadcast_to(alpha * l_ref[h, 0:1, :] + pv[HEAD_DIM:HEAD_DIM + 1, cols],
                                            (SUBLANES, TQ))

    def softmax_head(h, s_in, mc_in, p_out):
        m_old = m_ref[h, 0:1, :]
        m_new = jnp.maximum(m_old, mc_in[h, 0:1, :])
        m_use = jnp.where(m_new == NEG_INF, 0.0, m_new)
        alpha_ref[h] = jnp.broadcast_to(jnp.exp2(m_old - m_use), (SUBLANES, TQ))
        m_ref[h] = jnp.broadcast_to(m_new, (SUBLANES, TQ))
        cols = slice(h * TQ, (h + 1) * TQ)
        for blk in range(TK // EXP_ROWS):
            rows = slice(blk * EXP_ROWS, (blk + 1) * EXP_ROWS)
            p_out[rows, cols] = jnp.exp2(s_in[rows, cols] - m_use).astype(BF16)

    def stage(c, s_cur, mc_cur, p_cur, s_nxt, mc_nxt, p_prev):
        c_nxt = jnp.minimum(c + 1, nk - 1)
        pv_update(jnp.maximum(c - 1, 0), p_prev)
        bias_nxt = bias_ref[c_nxt]
        k_nxt = k_ref[0, c_nxt]
        for h in range(N_HEADS):
            logits_head(h, k_nxt, bias_nxt, s_nxt, mc_nxt)
            softmax_head(h, s_cur, mc_cur, p_cur)

    for h in range(N_HEADS):
        logits_head(h, k_ref[0, 0], bias_ref[0], s0_ref, mc0_ref)

    def pair_body(j, carry):
        c = 2 * j
        stage(c, s0_ref, mc0_ref, p0_ref, s1_ref, mc1_ref, p1_ref)

        @pl.when(c + 1 < nk)
        def _():
            stage(c + 1, s1_ref, mc1_ref, p1_ref, s0_ref, mc0_ref, p0_ref)

        return carry

    lax.fori_loop(0, (nk + 1) // 2, pair_body, 0)

    @pl.when((nk - 1) % 2 == 0)
    def _():
        pv_update(nk - 1, p0_ref)

    @pl.when((nk - 1) % 2 == 1)
    def _():
        pv_update(nk - 1, p1_ref)

    for h in range(N_HEADS):
        rows = slice(h * HEAD_DIM, (h + 1) * HEAD_DIM)
        acc_ref[rows, :] = acc_ref[rows, :] / l_ref[h, 0:1, :]
    yb_ref[0] = acc_ref[...].T.astype(BF16)


def _run_attn(qi, wi, ki, q, k, vt, k_sel):
    b, s, _ = q.shape
    nc = s // TK
    ki = ki.reshape(b, nc, TK, IDX_DIM)
    k = k.reshape(b, nc, TK, D_KV)
    rows = lambda width: pl.BlockSpec((1, TQ, width), lambda bi, i: (bi, i, 0))
    keys = lambda d0, d1: pl.BlockSpec((1, nc, d0, d1), lambda bi, i: (bi, 0, 0, 0))
    return pl.pallas_call(
        functools.partial(_attn_kernel, k_sel=k_sel),
        grid=(b, s // TQ),
        in_specs=[rows(IDX_HEADS * IDX_DIM), rows(LANES), keys(TK, IDX_DIM),
                  rows(D_ATTN), keys(TK, D_KV), keys(D_KV, TK)],
        out_specs=rows(D_ATTN),
        out_shape=jax.ShapeDtypeStruct((b, s, D_ATTN), BF16),
        scratch_shapes=[pltpu.VMEM((nc, TK, TQ), F32),
                        pltpu.VMEM((nc, TK, TQ), F32),
                        pltpu.VMEM((IDX_DIM, IDX_HEADS * TQ), BF16),
                        pltpu.VMEM((D_KV, N_HEADS * TQ), BF16),
                        pltpu.VMEM((TK, N_HEADS * TQ), F32),
                        pltpu.VMEM((TK, N_HEADS * TQ), F32),
                        pltpu.VMEM((TK, N_HEADS * TQ), BF16),
                        pltpu.VMEM((TK, N_HEADS * TQ), BF16),
                        pltpu.VMEM((N_HEADS, SUBLANES, TQ), F32),
                        pltpu.VMEM((N_HEADS, SUBLANES, TQ), F32),
                        pltpu.VMEM((N_HEADS, SUBLANES, TQ), F32),
                        pltpu.VMEM((N_HEADS, SUBLANES, TQ), F32),
                        pltpu.VMEM((N_HEADS, SUBLANES, TQ), F32),
                        pltpu.VMEM((D_ATTN, TQ), F32),
                        pltpu.VMEM((SUBLANES, TQ), F32)],
        compiler_params=pltpu.CompilerParams(
            dimension_semantics=("parallel", "parallel"), vmem_limit_bytes=VMEM_LIMIT),
        name="dsa_attention",
    )(qi, wi, ki, q, k, vt)


def _tail_kernel(x_ref, ya_ref, yb_ref, wg_ref, wbr_ref, wo_ref, wup_ref, wdn_ref,
                 g1_ref, b1_ref, g2_ref, b2_ref, out_ref, *, alpha):
    half = TM_TAIL // 2
    halves = (slice(0, half), slice(half, TM_TAIL))

    def mix_pre_ln(rows):
        x = x_ref[rows, :]
        xb = x.astype(BF16)
        merged = jnp.zeros((half, D_MODEL), F32)
        for n, y_ref in enumerate((ya_ref, yb_ref)):
            gate = _dot_nt(xb, wg_ref[n * D_MODEL:(n + 1) * D_MODEL, :])
            branch = jnp.dot(y_ref[rows, :], wbr_ref[n], preferred_element_type=F32)
            merged = merged + jax.nn.sigmoid(gate) * branch
        return alpha * x + jnp.dot(merged.astype(BF16), wo_ref[...], preferred_element_type=F32)

    def mlp_pre_ln(h):
        hb = h.astype(BF16)
        ff = jnp.zeros((half, D_MODEL), F32)
        for f in range(D_FF // FF_CHUNK):
            up = jnp.dot(hb, wup_ref[:, f * FF_CHUNK:(f + 1) * FF_CHUNK], preferred_element_type=F32)
            act = jnp.square(jnp.maximum(up, 0.0)).astype(BF16)
            ff = ff + jnp.dot(act, wdn_ref[f * FF_CHUNK:(f + 1) * FF_CHUNK, :], preferred_element_type=F32)
        return alpha * h + ff

    ln1 = lambda v: _layer_norm_rows(v, g1_ref[...], b1_ref[...])
    ln2 = lambda v: _layer_norm_rows(v, g2_ref[...], b2_ref[...])
    pre1_a = mix_pre_ln(halves[0])
    h_a = ln1(pre1_a)
    pre1_b = mix_pre_ln(halves[1])
    h_b = ln1(pre1_b)
    pre2_a = mlp_pre_ln(h_a)
    out_ref[halves[0], :] = ln2(pre2_a)
    pre2_b = mlp_pre_ln(h_b)
    out_ref[halves[1], :] = ln2(pre2_b)


def _run_tail(x2, ya2, yb2, wg, wbr, wo, wup, wdn, g1, b1, g2, b2, alpha):
    n, d = x2.shape
    const = lambda *shape: pl.BlockSpec(shape, lambda r: (0,) * len(shape),
                                        pipeline_mode=pl.Buffered(1))
    rows = lambda width: pl.BlockSpec((TM_TAIL, width), lambda r: (r, 0))
    return pl.pallas_call(
        functools.partial(_tail_kernel, alpha=alpha),
        grid=(n // TM_TAIL,),
        in_specs=[rows(d), rows(D_CONV), rows(D_ATTN),
                  const(N_BRANCH * d, d), const(N_BRANCH, D_CONV, d), const(d, d),
                  const(d, D_FF), const(D_FF, d),
                  const(1, d), const(1, d), const(1, d), const(1, d)],
        out_specs=rows(d),
        out_shape=jax.ShapeDtypeStruct((n, d), F32),
        compiler_params=pltpu.CompilerParams(
            dimension_semantics=("parallel",), vmem_limit_bytes=VMEM_LIMIT),
        name="merge_mlp",
    )(x2, ya2, yb2, wg, wbr, wo, wup, wdn, g1, b1, g2, b2)


def kernel(x, w_in, conv_w, idx_k_norm_g, idx_k_norm_b, w_branch, w_o, ln1_g, ln1_b, w_up, w_down, ln2_g, ln2_b):
    b, s, d = x.shape
    depth = w_in.shape[0]
    assert d == D_MODEL and s % TM_PROJ == 0 and (b * s) % TM_TAIL == 0 and TQ == TK
    k_sel = min(TOPK_MAX, s // 4)
    alpha = (2 * depth) ** 0.25
    for l in range(depth):
        wt = jnp.swapaxes(w_in[l], 0, 1)
        wa = wt[:OFF_KI].astype(BF16)
        wkw = jnp.concatenate(
            [wt[OFF_KI:OFF_WI], jnp.zeros((LANES - IDX_DIM, d), F32),
             wt[OFF_WI:OFF_GATES], jnp.zeros((LANES - IDX_HEADS, d), F32)], axis=0).astype(BF16)
        wg = wt[OFF_GATES:].astype(BF16)
        pad = lambda a: jnp.pad(a.astype(F32), (0, LANES - IDX_DIM)).reshape(1, LANES)
        ya, q, k, vt, qi, ki, wi = _run_proj(x, wa, wkw, conv_w[l], pad(idx_k_norm_g[l]), pad(idx_k_norm_b[l]))
        yb = _run_attn(qi, wi, ki, q, k, vt, k_sel)
        row = lambda a: a.astype(F32).reshape(1, d)
        out = _run_tail(x.reshape(b * s, d), ya.reshape(b * s, D_CONV), yb.reshape(b * s, D_ATTN),
                        wg, w_branch[l].astype(BF16), w_o[l].astype(BF16),
                        w_up[l].astype(BF16), w_down[l].astype(BF16),
                        row(ln1_g[l]), row(ln1_b[l]), row(ln2_g[l]), row(ln2_b[l]), alpha)
        x = out.reshape(b, s, d)
    return x
```

```python
import functools

import jax
import jax.numpy as jnp
from jax import lax
from jax.experimental import pallas as pl
from jax.experimental.pallas import tpu as pltpu

D_MODEL = 1024
D_CONV = 512
CONV_WIDTH = 3
N_HEADS = 8
N_KV_HEADS = 2
HEAD_DIM = 64
D_ATTN = N_HEADS * HEAD_DIM
D_KV = N_KV_HEADS * HEAD_DIM
IDX_HEADS = 8
IDX_DIM = 64
TOPK_MAX = 256
N_BRANCH = 2
D_FF = 4 * D_MODEL
LN_EPS = 1e-5
IDX_SCALE = (IDX_DIM ** -0.5) * (IDX_HEADS ** -0.5)
ATTN_SCALE = HEAD_DIM ** -0.5
LOG2_E = 1.4426950408889634
Q_SCALE = ATTN_SCALE * LOG2_E

OFF_Q = 3 * D_CONV
OFF_K = OFF_Q + D_ATTN
OFF_V = OFF_K + D_KV
OFF_QI = OFF_V + D_KV
OFF_KI = OFF_QI + IDX_HEADS * IDX_DIM
OFF_WI = OFF_KI + IDX_DIM
OFF_GATES = OFF_WI + IDX_HEADS

LANES = 128
SUBLANES = 8
BF16_SUBLANES = 16
VMEM_LIMIT = 56 * 1024 * 1024

TM_PROJ = 1024
TQ = 256
TK = 256
TM_TAIL = 1024
TAIL_PARTS = 4
FF_CHUNK = 1024
COUNT_CHAINS = 4
EXP_ROWS = 32

F32 = jnp.float32
BF16 = jnp.bfloat16
NEG_INF = float("-inf")
F32_MAX = float(jnp.finfo(jnp.float32).max)
INT_MIN = -2 ** 31
KEY_BITS = 32


def _dot_nt(a, b_t):
    return lax.dot_general(a, b_t, (((1,), (1,)), ((), ())), preferred_element_type=F32)


def _layer_norm_rows(x, g, b):
    mu = jnp.mean(x, axis=-1, keepdims=True)
    xc = x - mu
    var = jnp.mean(xc * xc, axis=-1, keepdims=True)
    return xc * lax.rsqrt(var + LN_EPS) * g + b


def _proj_kernel(x_ref, wa_ref, wkw_ref, convw_ref, kg_ref, kb_ref,
                 ya_ref, q_ref, k_ref, vt_ref, qi_ref, ki_ref, wi_ref,
                 carry_ref):
    @pl.when(pl.program_id(1) == 0)
    def _():
        carry_ref[...] = jnp.zeros_like(carry_ref)

    xb = x_ref[0].astype(BF16)

    def proj(lo, hi):
        return _dot_nt(xb, wa_ref[lo:hi, :])

    zkw = _dot_nt(xb, wkw_ref[...])
    wi_ref[0] = zkw[:, LANES:]
    zk = zkw[:, :LANES]
    valid = lax.broadcasted_iota(jnp.int32, zk.shape, 1) < IDX_DIM
    mu = jnp.sum(jnp.where(valid, zk, 0.0), axis=-1, keepdims=True) * (1.0 / IDX_DIM)
    xc = jnp.where(valid, zk - mu, 0.0)
    var = jnp.sum(xc * xc, axis=-1, keepdims=True) * (1.0 / IDX_DIM)
    kin = xc * lax.rsqrt(var + LN_EPS) * kg_ref[...] + kb_ref[...]
    ki_ref[0] = kin[:, :IDX_DIM].astype(BF16)

    vt = proj(OFF_V, OFF_QI).T
    for c in range(TM_PROJ // TK):
        vt_ref[0, c] = vt[:, c * TK:(c + 1) * TK].astype(BF16)

    cu = proj(D_CONV, 2 * D_CONV) * proj(2 * D_CONV, 3 * D_CONV)
    prev = carry_ref[...]
    row = lax.broadcasted_iota(jnp.int32, cu.shape, 0)
    cu1 = jnp.where(row == 0, prev[SUBLANES - 1:SUBLANES, :], pltpu.roll(cu, 1, axis=0))
    cu2 = jnp.where(row == 0, prev[SUBLANES - 2:SUBLANES - 1, :],
                    jnp.where(row == 1, prev[SUBLANES - 1:SUBLANES, :], pltpu.roll(cu, 2, axis=0)))
    carry_ref[...] = cu[TM_PROJ - SUBLANES:, :]
    cw = convw_ref[...]
    conv = cw[0:1, :] * cu2 + cw[1:2, :] * cu1 + cw[2:3, :] * cu
    ya_ref[0] = (proj(0, D_CONV) * conv).astype(BF16)

    k_ref[0] = proj(OFF_K, OFF_V).astype(BF16)
    q_ref[0] = (proj(OFF_Q, OFF_K) * Q_SCALE).astype(BF16)
    qi_ref[0] = proj(OFF_QI, OFF_KI).astype(BF16)


def _run_proj(x, wa, wkw, conv_w, kg, kb):
    b, s, d = x.shape
    nc = s // TK
    cpt = TM_PROJ // TK
    const = lambda *shape: pl.BlockSpec(shape, lambda bi, j: (0,) * len(shape),
                                        pipeline_mode=pl.Buffered(1))
    rows = lambda width: pl.BlockSpec((1, TM_PROJ, width), lambda bi, j: (bi, j, 0))
    return pl.pallas_call(
        _proj_kernel,
        grid=(b, s // TM_PROJ),
        in_specs=[rows(d), const(OFF_KI, d), const(2 * LANES, d), const(CONV_WIDTH, D_CONV),
                  const(1, LANES), const(1, LANES)],
        out_specs=[rows(D_CONV), rows(D_ATTN), rows(D_KV),
                   pl.BlockSpec((1, cpt, D_KV, TK), lambda bi, j: (bi, j, 0, 0)),
                   rows(IDX_HEADS * IDX_DIM), rows(IDX_DIM), rows(LANES)],
        out_shape=[jax.ShapeDtypeStruct((b, s, D_CONV), BF16),
                   jax.ShapeDtypeStruct((b, s, D_ATTN), BF16),
                   jax.ShapeDtypeStruct((b, s, D_KV), BF16),
                   jax.ShapeDtypeStruct((b, nc, D_KV, TK), BF16),
                   jax.ShapeDtypeStruct((b, s, IDX_HEADS * IDX_DIM), BF16),
                   jax.ShapeDtypeStruct((b, s, IDX_DIM), BF16),
                   jax.ShapeDtypeStruct((b, s, LANES), F32)],
        scratch_shapes=[pltpu.VMEM((SUBLANES, D_CONV), F32)],
        compiler_params=pltpu.CompilerParams(
            dimension_semantics=("parallel", "arbitrary"), vmem_limit_bytes=VMEM_LIMIT),
        name="proj_mixer_a",
    )(x, wa, wkw, conv_w, kg, kb)


def _sortable_to_f32(x):
    bits = x ^ ((x >> 31) & jnp.int32(0x7FFFFFFF))
    return pltpu.bitcast(bits, F32)


def _attn_kernel(qi_ref, wi_ref, ki_ref, q_ref, k_ref, vt_ref, yb_ref,
                 score_ref, bias_ref, qit_ref, qpad_ref, s0_ref, s1_ref, p0_ref, p1_ref,
                 mc0_ref, mc1_ref, alpha_ref, m_ref, l_ref, acc_ref, thr_ref, tau_ref, cntp_ref, *, k_sel):
    i = pl.program_id(1)
    nk = i + 1
    kf = float(k_sel)
    rep_heads = N_HEADS // N_KV_HEADS
    n_groups = TK // SUBLANES
    n_lane_groups = TQ // LANES

    qit = qi_ref[0].astype(F32).T.astype(BF16)
    for h in range(IDX_HEADS):
        qit_ref[:, h * TQ:(h + 1) * TQ] = qit[h * IDX_DIM:(h + 1) * IDX_DIM, :]
    wt = wi_ref[0].T[:IDX_HEADS, :]
    qt = q_ref[0].astype(F32).T.astype(BF16)
    zero_half = jnp.zeros((HEAD_DIM, TQ), BF16)
    for h in range(N_HEADS):
        g = h // rep_heads
        for gg in range(N_KV_HEADS):
            qpad_ref[gg * HEAD_DIM:(gg + 1) * HEAD_DIM, h * TQ:(h + 1) * TQ] = (
                qt[h * HEAD_DIM:(h + 1) * HEAD_DIM, :] if gg == g else zero_half)

    row = lax.broadcasted_iota(jnp.int32, (TK, TQ), 0)
    col = lax.broadcasted_iota(jnp.int32, (TK, TQ), 1)
    row_minus_col = row - col

    def score_body(c, carry):
        kic = ki_ref[0, c]
        lg_all = jnp.dot(kic, qit_ref[...], preferred_element_type=F32)
        acc = jnp.zeros((TK, TQ), F32)
        for h in range(IDX_HEADS):
            lg = lg_all[:, h * TQ:(h + 1) * TQ]
            acc = acc + wt[h:h + 1, :] * jnp.maximum(lg, 0.0)
        noncausal = row_minus_col > (i - c) * TQ
        score_ref[c] = jnp.where(noncausal, NEG_INF, acc * IDX_SCALE)
        return carry

    lax.fori_loop(0, nk, score_body, 0)

    def count_where(pred, thr):
        thr_ref[...] = jnp.broadcast_to(thr, (SUBLANES, TQ))
        thr_g = [thr_ref[:, g * LANES:(g + 1) * LANES] for g in range(n_lane_groups)]

        def chunk_counts(c, accs, diagonal):
            accs = [list(a) for a in accs]
            for r in range(n_groups):
                for g in range(n_lane_groups):
                    if diagonal and r * SUBLANES >= (g + 1) * LANES:
                        continue
                    blk = score_ref[c, r * SUBLANES:(r + 1) * SUBLANES, g * LANES:(g + 1) * LANES]
                    a = accs[g][r % COUNT_CHAINS]
                    accs[g][r % COUNT_CHAINS] = jnp.where(pred(blk, thr_g[g]), a + 1.0, a)
            return tuple(tuple(a) for a in accs)

        zero = jnp.zeros((SUBLANES, LANES), F32)
        accs = ((zero,) * COUNT_CHAINS,) * n_lane_groups
        accs = lax.fori_loop(0, nk - 1, lambda c, a: chunk_counts(c, a, False), accs)
        accs = chunk_counts(nk - 1, accs, True)
        per_group = [sum(a[1:], a[0]) for a in accs]
        return jnp.sum(jnp.concatenate(per_group, axis=1), axis=0, keepdims=True)

    def count_ge(cand_key):
        return count_where(lambda blk, thr: blk >= thr, _sortable_to_f32(cand_key))

    def bit_body(b, carry):
        prefix, cntp = carry
        cand_key = prefix | jnp.left_shift(jnp.int32(1), KEY_BITS - 2 - b)
        cnt = count_ge(cand_key)
        ok = cnt >= kf
        return jnp.where(ok, cand_key, prefix), jnp.where(ok, cnt, cntp)

    tau_ref[...] = jnp.full(tau_ref.shape, -F32_MAX, F32)
    cntp_ref[...] = jnp.zeros(cntp_ref.shape, F32)

    @pl.when(nk * TK > k_sel)
    def _():
        cnt0 = count_ge(jnp.zeros((1, TQ), jnp.int32))
        ok0 = cnt0 >= kf
        prefix0 = jnp.where(ok0, jnp.int32(0), jnp.int32(INT_MIN))
        prefix, cntp = lax.fori_loop(0, KEY_BITS - 1, bit_body, (prefix0, jnp.where(ok0, cnt0, 0.0)))
        tau_found = jnp.where(prefix == INT_MIN, -F32_MAX, _sortable_to_f32(prefix))
        tau_ref[...] = jnp.broadcast_to(tau_found, tau_ref.shape)
        cntp_ref[...] = jnp.broadcast_to(cntp, cntp_ref.shape)

    tau = tau_ref[0:1, :]
    cntp = cntp_ref[0:1, :]

    def bias_body(c, carry):
        bias_ref[c] = jnp.where(score_ref[c] >= tau, 0.0, NEG_INF)
        return carry

    lax.fori_loop(0, nk, bias_body, 0)

    @pl.when(jnp.max(cntp) > kf)
    def _():
        need = kf - count_where(lambda blk, thr: blk > thr, tau)
        r = lax.broadcasted_iota(jnp.int32, (TK, TK), 0)
        cc = lax.broadcasted_iota(jnp.int32, (TK, TK), 1)
        before = jnp.where(cc < r, 1.0, 0.0).astype(BF16)

        def tie_body(c, seen):
            sc = score_ref[c]
            eq = sc == tau
            eqf = jnp.where(eq, 1.0, 0.0)
            rank = jnp.dot(before, eqf.astype(BF16), preferred_element_type=F32) + seen
            keep = (sc > tau) | (eq & (rank < need))
            bias_ref[c] = jnp.where(keep, 0.0, NEG_INF)
            return seen + jnp.sum(eqf, axis=0, keepdims=True)

        lax.fori_loop(0, nk, tie_body, jnp.zeros((1, TQ), F32))

    m_ref[...] = jnp.full(m_ref.shape, NEG_INF, F32)
    l_ref[...] = jnp.zeros(l_ref.shape, F32)
    acc_ref[...] = jnp.zeros(acc_ref.shape, F32)
    alpha_ref[...] = jnp.ones(alpha_ref.shape, F32)
    p1_ref[...] = jnp.zeros(p1_ref.shape, BF16)
    ones_rows = jnp.ones((BF16_SUBLANES, TK), BF16)

    def logits_head(h, kc, bias, s_out, mc_out):
        cols = slice(h * TQ, (h + 1) * TQ)
        s = jnp.dot(kc, qpad_ref[:, cols], preferred_element_type=F32) + bias
        s_out[:, cols] = s
        mc_out[h] = jnp.broadcast_to(jnp.max(s, axis=0, keepdims=True), (SUBLANES, TQ))

    def pv_update(c, p_in):
        for g in range(N_KV_HEADS):
            lanes = slice(g * rep_heads * TQ, (g + 1) * rep_heads * TQ)
            lhs = jnp.concatenate([vt_ref[0, c, g * HEAD_DIM:(g + 1) * HEAD_DIM, :], ones_rows], axis=0)
            pv = jnp.dot(lhs, p_in[:, lanes], preferred_element_type=F32)
            for hh in range(rep_heads):
                h = g * rep_heads + hh
                rows = slice(h * HEAD_DIM, (h + 1) * HEAD_DIM)
                cols = slice(hh * TQ, (hh + 1) * TQ)
                alpha = alpha_ref[h, 0:1, :]
                acc_ref[rows, :] = alpha * acc_ref[rows, :] + pv[:HEAD_DIM, cols]
                l_ref[h] = jnp.broadcast_to(alpha * l_ref[h, 0:1, :] + pv[HEAD_DIM:HEAD_DIM + 1, cols],
                                            (SUBLANES, TQ))

    def softmax_head(h, s_in, mc_in, p_out):
        m_old = m_ref[h, 0:1, :]
        m_new = jnp.maximum(m_old, mc_in[h, 0:1, :])
        m_use = jnp.where(m_new == NEG_INF, 0.0, m_new)
        alpha_ref[h] = jnp.broadcast_to(jnp.exp2(m_old - m_use), (SUBLANES, TQ))
        m_ref[h] = jnp.broadcast_to(m_new, (SUBLANES, TQ))
        cols = slice(h * TQ, (h + 1) * TQ)
        for blk in range(TK // EXP_ROWS):
            rows = slice(blk * EXP_ROWS, (blk + 1) * EXP_ROWS)
            p_out[rows, cols] = jnp.exp2(s_in[rows, cols] - m_use).astype(BF16)

    def stage(c, s_cur, mc_cur, p_cur, s_nxt, mc_nxt, p_prev):
        c_nxt = jnp.minimum(c + 1, nk - 1)
        pv_update(jnp.maximum(c - 1, 0), p_prev)
        bias_nxt = bias_ref[c_nxt]
        k_nxt = k_ref[0, c_nxt]
        for h in range(N_HEADS):
            logits_head(h, k_nxt, bias_nxt, s_nxt, mc_nxt)
            softmax_head(h, s_cur, mc_cur, p_cur)

    for h in range(N_HEADS):
        logits_head(h, k_ref[0, 0], bias_ref[0], s0_ref, mc0_ref)

    def pair_body(j, carry):
        c = 2 * j
        stage(c, s0_ref, mc0_ref, p0_ref, s1_ref, mc1_ref, p1_ref)

        @pl.when(c + 1 < nk)
        def _():
            stage(c + 1, s1_ref, mc1_ref, p1_ref, s0_ref, mc0_ref, p0_ref)

        return carry

    lax.fori_loop(0, (nk + 1) // 2, pair_body, 0)

    @pl.when((nk - 1) % 2 == 0)
    def _():
        pv_update(nk - 1, p0_ref)

    @pl.when((nk - 1) % 2 == 1)
    def _():
        pv_update(nk - 1, p1_ref)

    for h in range(N_HEADS):
        rows = slice(h * HEAD_DIM, (h + 1) * HEAD_DIM)
        acc_ref[rows, :] = acc_ref[rows, :] / l_ref[h, 0:1, :]
    yb_ref[0] = acc_ref[...].T.astype(BF16)


def _run_attn(qi, wi, ki, q, k, vt, k_sel):
    b, s, _ = q.shape
    nc = s // TK
    ki = ki.reshape(b, nc, TK, IDX_DIM)
    k = k.reshape(b, nc, TK, D_KV)
    rows = lambda width: pl.BlockSpec((1, TQ, width), lambda bi, i: (bi, i, 0))
    keys = lambda d0, d1: pl.BlockSpec((1, nc, d0, d1), lambda bi, i: (bi, 0, 0, 0))
    return pl.pallas_call(
        functools.partial(_attn_kernel, k_sel=k_sel),
        grid=(b, s // TQ),
        in_specs=[rows(IDX_HEADS * IDX_DIM), rows(LANES), keys(TK, IDX_DIM),
                  rows(D_ATTN), keys(TK, D_KV), keys(D_KV, TK)],
        out_specs=rows(D_ATTN),
        out_shape=jax.ShapeDtypeStruct((b, s, D_ATTN), BF16),
        scratch_shapes=[pltpu.VMEM((nc, TK, TQ), F32),
                        pltpu.VMEM((nc, TK, TQ), F32),
                        pltpu.VMEM((IDX_DIM, IDX_HEADS * TQ), BF16),
                        pltpu.VMEM((D_KV, N_HEADS * TQ), BF16),
                        pltpu.VMEM((TK, N_HEADS * TQ), F32),
                        pltpu.VMEM((TK, N_HEADS * TQ), F32),
                        pltpu.VMEM((TK, N_HEADS * TQ), BF16),
                        pltpu.VMEM((TK, N_HEADS * TQ), BF16),
                        pltpu.VMEM((N_HEADS, SUBLANES, TQ), F32),
                        pltpu.VMEM((N_HEADS, SUBLANES, TQ), F32),
                        pltpu.VMEM((N_HEADS, SUBLANES, TQ), F32),
                        pltpu.VMEM((N_HEADS, SUBLANES, TQ), F32),
                        pltpu.VMEM((N_HEADS, SUBLANES, TQ), F32),
                        pltpu.VMEM((D_ATTN, TQ), F32),
                        pltpu.VMEM((SUBLANES, TQ), F32),
                        pltpu.VMEM((SUBLANES, TQ), F32),
                        pltpu.VMEM((SUBLANES, TQ), F32)],
        compiler_params=pltpu.CompilerParams(
            dimension_semantics=("parallel", "parallel"), vmem_limit_bytes=VMEM_LIMIT),
        name="dsa_attention",
    )(qi, wi, ki, q, k, vt)


def _tail_kernel(x_ref, ya_ref, yb_ref, wg_ref, wbr_ref, wo_ref, wup_ref, wdn_ref,
                 g1_ref, b1_ref, g2_ref, b2_ref, out_ref, *, alpha):
    part = TM_TAIL // TAIL_PARTS
    parts = [slice(p * part, (p + 1) * part) for p in range(TAIL_PARTS)]

    def mix_pre_ln(rows):
        x = x_ref[rows, :]
        xb = x.astype(BF16)
        merged = jnp.zeros((part, D_MODEL), F32)
        for n, y_ref in enumerate((ya_ref, yb_ref)):
            gate = _dot_nt(xb, wg_ref[n * D_MODEL:(n + 1) * D_MODEL, :])
            branch = jnp.dot(y_ref[rows, :], wbr_ref[n], preferred_element_type=F32)
            merged = merged + jax.nn.sigmoid(gate) * branch
        return alpha * x + jnp.dot(merged.astype(BF16), wo_ref[...], preferred_element_type=F32)

    def mlp_pre_ln(h):
        hb = h.astype(BF16)
        ff = jnp.zeros((part, D_MODEL), F32)
        for f in range(D_FF // FF_CHUNK):
            up = jnp.dot(hb, wup_ref[:, f * FF_CHUNK:(f + 1) * FF_CHUNK], preferred_element_type=F32)
            act = jnp.square(jnp.maximum(up, 0.0)).astype(BF16)
            ff = ff + jnp.dot(act, wdn_ref[f * FF_CHUNK:(f + 1) * FF_CHUNK, :], preferred_element_type=F32)
        return alpha * h + ff

    ln1 = lambda v: _layer_norm_rows(v, g1_ref[...], b1_ref[...])
    ln2 = lambda v: _layer_norm_rows(v, g2_ref[...], b2_ref[...])
    hs = [ln1(mix_pre_ln(rows)) for rows in parts]
    for rows, h in zip(parts, hs):
        out_ref[rows, :] = ln2(mlp_pre_ln(h))


def _run_tail(x2, ya2, yb2, wg, wbr, wo, wup, wdn, g1, b1, g2, b2, alpha):
    n, d = x2.shape
    const = lambda *shape: pl.BlockSpec(shape, lambda r: (0,) * len(shape),
                                        pipeline_mode=pl.Buffered(1))
    rows = lambda width: pl.BlockSpec((TM_TAIL, width), lambda r: (r, 0))
    return pl.pallas_call(
        functools.partial(_tail_kernel, alpha=alpha),
        grid=(n // TM_TAIL,),
        in_specs=[rows(d), rows(D_CONV), rows(D_ATTN),
                  const(N_BRANCH * d, d), const(N_BRANCH, D_CONV, d), const(d, d),
                  const(d, D_FF), const(D_FF, d),
                  const(1, d), const(1, d), const(1, d), const(1, d)],
        out_specs=rows(d),
        out_shape=jax.ShapeDtypeStruct((n, d), F32),
        compiler_params=pltpu.CompilerParams(
            dimension_semantics=("parallel",), vmem_limit_bytes=VMEM_LIMIT),
        name="merge_mlp",
    )(x2, ya2, yb2, wg, wbr, wo, wup, wdn, g1, b1, g2, b2)


def kernel(x, w_in, conv_w, idx_k_norm_g, idx_k_norm_b, w_branch, w_o, ln1_g, ln1_b, w_up, w_down, ln2_g, ln2_b):
    b, s, d = x.shape
    depth = w_in.shape[0]
    assert d == D_MODEL and s % TM_PROJ == 0 and (b * s) % TM_TAIL == 0 and TQ == TK
    k_sel = min(TOPK_MAX, s // 4)
    alpha = (2 * depth) ** 0.25
    for l in range(depth):
        wt = jnp.swapaxes(w_in[l], 0, 1)
        wa = wt[:OFF_KI].astype(BF16)
        wkw = jnp.concatenate(
            [wt[OFF_KI:OFF_WI], jnp.zeros((LANES - IDX_DIM, d), F32),
             wt[OFF_WI:OFF_GATES], jnp.zeros((LANES - IDX_HEADS, d), F32)], axis=0).astype(BF16)
        wg = wt[OFF_GATES:].astype(BF16)
        pad = lambda a: jnp.pad(a.astype(F32), (0, LANES - IDX_DIM)).reshape(1, LANES)
        ya, q, k, vt, qi, ki, wi = _run_proj(x, wa, wkw, conv_w[l], pad(idx_k_norm_g[l]), pad(idx_k_norm_b[l]))
        yb = _run_attn(qi, wi, ki, q, k, vt, k_sel)
        row = lambda a: a.astype(F32).reshape(1, d)
        out = _run_tail(x.reshape(b * s, d), ya.reshape(b * s, D_CONV), yb.reshape(b * s, D_ATTN),
                        wg, w_branch[l].astype(BF16), w_o[l].astype(BF16),
                        w_up[l].astype(BF16), w_down[l].astype(BF16),
                        row(ln1_g[l]), row(ln1_b[l]), row(ln2_g[l]), row(ln2_b[l]), alpha)
        x = out.reshape(b, s, d)
    return x
```

```python
import functools

import jax
import jax.numpy as jnp
from jax import lax
from jax.experimental import pallas as pl
from jax.experimental.pallas import tpu as pltpu

D_MODEL = 1024
D_CONV = 512
CONV_WIDTH = 3
N_HEADS = 8
N_KV_HEADS = 2
HEAD_DIM = 64
D_ATTN = N_HEADS * HEAD_DIM
D_KV = N_KV_HEADS * HEAD_DIM
IDX_HEADS = 8
IDX_DIM = 64
TOPK_MAX = 256
N_BRANCH = 2
D_FF = 4 * D_MODEL
LN_EPS = 1e-5
IDX_SCALE = (IDX_DIM ** -0.5) * (IDX_HEADS ** -0.5)
ATTN_SCALE = HEAD_DIM ** -0.5
LOG2_E = 1.4426950408889634
Q_SCALE = ATTN_SCALE * LOG2_E

OFF_Q = 3 * D_CONV
OFF_K = OFF_Q + D_ATTN
OFF_V = OFF_K + D_KV
OFF_QI = OFF_V + D_KV
OFF_KI = OFF_QI + IDX_HEADS * IDX_DIM
OFF_WI = OFF_KI + IDX_DIM
OFF_GATES = OFF_WI + IDX_HEADS

LANES = 128
SUBLANES = 8
BF16_SUBLANES = 16
VMEM_LIMIT = 56 * 1024 * 1024

TM_PROJ = 1024
TQ = 256
TK = 256
TM_TAIL = 512
TAIL_PARTS = 2
FF_CHUNK = 1024
COUNT_CHAINS = 4
EXP_ROWS = 32

F32 = jnp.float32
BF16 = jnp.bfloat16
NEG_INF = float("-inf")
F32_MAX = float(jnp.finfo(jnp.float32).max)
INT_MIN = -2 ** 31
KEY_BITS = 32


def _dot_nt(a, b_t):
    return lax.dot_general(a, b_t, (((1,), (1,)), ((), ())), preferred_element_type=F32)


def _layer_norm_rows(x, g, b):
    mu = jnp.mean(x, axis=-1, keepdims=True)
    xc = x - mu
    var = jnp.mean(xc * xc, axis=-1, keepdims=True)
    return xc * lax.rsqrt(var + LN_EPS) * g + b


def _proj_kernel(x_ref, wa_ref, wkw_ref, convw_ref, kg_ref, kb_ref,
                 ya_ref, q_ref, k_ref, vt_ref, qi_ref, ki_ref, wi_ref,
                 carry_ref):
    @pl.when(pl.program_id(1) == 0)
    def _():
        carry_ref[...] = jnp.zeros_like(carry_ref)

    xb = x_ref[0].astype(BF16)

    def proj(lo, hi):
        return _dot_nt(xb, wa_ref[lo:hi, :])

    zkw = _dot_nt(xb, wkw_ref[...])
    wi_ref[0] = zkw[:, LANES:]
    zk = zkw[:, :LANES]
    valid = lax.broadcasted_iota(jnp.int32, zk.shape, 1) < IDX_DIM
    mu = jnp.sum(jnp.where(valid, zk, 0.0), axis=-1, keepdims=True) * (1.0 / IDX_DIM)
    xc = jnp.where(valid, zk - mu, 0.0)
    var = jnp.sum(xc * xc, axis=-1, keepdims=True) * (1.0 / IDX_DIM)
    kin = xc * lax.rsqrt(var + LN_EPS) * kg_ref[...] + kb_ref[...]
    ki_ref[0] = kin[:, :IDX_DIM].astype(BF16)

    vt = proj(OFF_V, OFF_QI).T
    for c in range(TM_PROJ // TK):
        vt_ref[0, c] = vt[:, c * TK:(c + 1) * TK].astype(BF16)

    cu = proj(D_CONV, 2 * D_CONV) * proj(2 * D_CONV, 3 * D_CONV)
    prev = carry_ref[...]
    row = lax.broadcasted_iota(jnp.int32, cu.shape, 0)
    cu1 = jnp.where(row == 0, prev[SUBLANES - 1:SUBLANES, :], pltpu.roll(cu, 1, axis=0))
    cu2 = jnp.where(row == 0, prev[SUBLANES - 2:SUBLANES - 1, :],
                    jnp.where(row == 1, prev[SUBLANES - 1:SUBLANES, :], pltpu.roll(cu, 2, axis=0)))
    carry_ref[...] = cu[TM_PROJ - SUBLANES:, :]
    cw = convw_ref[...]
    conv = cw[0:1, :] * cu2 + cw[1:2, :] * cu1 + cw[2:3, :] * cu
    ya_ref[0] = (proj(0, D_CONV) * conv).astype(BF16)

    k_ref[0] = proj(OFF_K, OFF_V).astype(BF16)
    q_ref[0] = (proj(OFF_Q, OFF_K) * Q_SCALE).astype(BF16)
    qi_ref[0] = proj(OFF_QI, OFF_KI).astype(BF16)


def _run_proj(x, wa, wkw, conv_w, kg, kb):
    b, s, d = x.shape
    nc = s // TK
    cpt = TM_PROJ // TK
    const = lambda *shape: pl.BlockSpec(shape, lambda bi, j: (0,) * len(shape),
                                        pipeline_mode=pl.Buffered(1))
    rows = lambda width: pl.BlockSpec((1, TM_PROJ, width), lambda bi, j: (bi, j, 0))
    return pl.pallas_call(
        _proj_kernel,
        grid=(b, s // TM_PROJ),
        in_specs=[rows(d), const(OFF_KI, d), const(2 * LANES, d), const(CONV_WIDTH, D_CONV),
                  const(1, LANES), const(1, LANES)],
        out_specs=[rows(D_CONV), rows(D_ATTN), rows(D_KV),
                   pl.BlockSpec((1, cpt, D_KV, TK), lambda bi, j: (bi, j, 0, 0)),
                   rows(IDX_HEADS * IDX_DIM), rows(IDX_DIM), rows(LANES)],
        out_shape=[jax.ShapeDtypeStruct((b, s, D_CONV), BF16),
                   jax.ShapeDtypeStruct((b, s, D_ATTN), BF16),
                   jax.ShapeDtypeStruct((b, s, D_KV), BF16),
                   jax.ShapeDtypeStruct((b, nc, D_KV, TK), BF16),
                   jax.ShapeDtypeStruct((b, s, IDX_HEADS * IDX_DIM), BF16),
                   jax.ShapeDtypeStruct((b, s, IDX_DIM), BF16),
                   jax.ShapeDtypeStruct((b, s, LANES), F32)],
        scratch_shapes=[pltpu.VMEM((SUBLANES, D_CONV), F32)],
        compiler_params=pltpu.CompilerParams(
            dimension_semantics=("parallel", "arbitrary"), vmem_limit_bytes=VMEM_LIMIT),
        name="proj_mixer_a",
    )(x, wa, wkw, conv_w, kg, kb)


def _sortable_to_f32(x):
    bits = x ^ ((x >> 31) & jnp.int32(0x7FFFFFFF))
    return pltpu.bitcast(bits, F32)


def _attn_kernel(qi_ref, wi_ref, ki_ref, q_ref, k_ref, vt_ref, yb_ref,
                 score_ref, bias_ref, qit_ref, qpad_ref, s0_ref, s1_ref, p0_ref, p1_ref,
                 mc0_ref, mc1_ref, alpha_ref, m_ref, l_ref, acc_ref, thr_ref, tau_ref, cntp_ref, *, k_sel):
    i = pl.program_id(1)
    nk = i + 1
    kf = float(k_sel)
    rep_heads = N_HEADS // N_KV_HEADS
    n_groups = TK // SUBLANES
    n_lane_groups = TQ // LANES

    qit = qi_ref[0].astype(F32).T.astype(BF16)
    for h in range(IDX_HEADS):
        qit_ref[:, h * TQ:(h + 1) * TQ] = qit[h * IDX_DIM:(h + 1) * IDX_DIM, :]
    wt = wi_ref[0].T[:IDX_HEADS, :]
    qt = q_ref[0].astype(F32).T.astype(BF16)
    zero_half = jnp.zeros((HEAD_DIM, TQ), BF16)
    for h in range(N_HEADS):
        g = h // rep_heads
        for gg in range(N_KV_HEADS):
            qpad_ref[gg * HEAD_DIM:(gg + 1) * HEAD_DIM, h * TQ:(h + 1) * TQ] = (
                qt[h * HEAD_DIM:(h + 1) * HEAD_DIM, :] if gg == g else zero_half)

    row = lax.broadcasted_iota(jnp.int32, (TK, TQ), 0)
    col = lax.broadcasted_iota(jnp.int32, (TK, TQ), 1)
    row_minus_col = row - col

    def score_body(c, carry):
        kic = ki_ref[0, c]
        lg_all = jnp.dot(kic, qit_ref[...], preferred_element_type=F32)
        acc = jnp.zeros((TK, TQ), F32)
        for h in range(IDX_HEADS):
            lg = lg_all[:, h * TQ:(h + 1) * TQ]
            acc = acc + wt[h:h + 1, :] * jnp.maximum(lg, 0.0)
        noncausal = row_minus_col > (i - c) * TQ
        score_ref[c] = jnp.where(noncausal, NEG_INF, acc * IDX_SCALE)
        return carry

    lax.fori_loop(0, nk, score_body, 0)

    def count_where(pred, thr):
        thr_ref[...] = jnp.broadcast_to(thr, (SUBLANES, TQ))
        thr_g = [thr_ref[:, g * LANES:(g + 1) * LANES] for g in range(n_lane_groups)]

        def chunk_counts(c, accs, diagonal):
            accs = [list(a) for a in accs]
            for r in range(n_groups):
                for g in range(n_lane_groups):
                    if diagonal and r * SUBLANES >= (g + 1) * LANES:
                        continue
                    blk = score_ref[c, r * SUBLANES:(r + 1) * SUBLANES, g * LANES:(g + 1) * LANES]
                    a = accs[g][r % COUNT_CHAINS]
                    accs[g][r % COUNT_CHAINS] = jnp.where(pred(blk, thr_g[g]), a + 1.0, a)
            return tuple(tuple(a) for a in accs)

        zero = jnp.zeros((SUBLANES, LANES), F32)
        accs = ((zero,) * COUNT_CHAINS,) * n_lane_groups
        accs = lax.fori_loop(0, nk - 1, lambda c, a: chunk_counts(c, a, False), accs)
        accs = chunk_counts(nk - 1, accs, True)
        per_group = [sum(a[1:], a[0]) for a in accs]
        return jnp.sum(jnp.concatenate(per_group, axis=1), axis=0, keepdims=True)

    def count_ge(cand_key):
        return count_where(lambda blk, thr: blk >= thr, _sortable_to_f32(cand_key))

    def bit_body(b, carry):
        prefix, cntp = carry
        cand_key = prefix | jnp.left_shift(jnp.int32(1), KEY_BITS - 2 - b)
        cnt = count_ge(cand_key)
        ok = cnt >= kf
        return jnp.where(ok, cand_key, prefix), jnp.where(ok, cnt, cntp)

    tau_ref[...] = jnp.full(tau_ref.shape, -F32_MAX, F32)
    cntp_ref[...] = jnp.zeros(cntp_ref.shape, F32)

    @pl.when(nk * TK > k_sel)
    def _():
        cnt0 = count_ge(jnp.zeros((1, TQ), jnp.int32))
        ok0 = cnt0 >= kf
        prefix0 = jnp.where(ok0, jnp.int32(0), jnp.int32(INT_MIN))
        prefix, cntp = lax.fori_loop(0, KEY_BITS - 1, bit_body, (prefix0, jnp.where(ok0, cnt0, 0.0)))
        tau_found = jnp.where(prefix == INT_MIN, -F32_MAX, _sortable_to_f32(prefix))
        tau_ref[...] = jnp.broadcast_to(tau_found, tau_ref.shape)
        cntp_ref[...] = jnp.broadcast_to(cntp, cntp_ref.shape)

    tau = tau_ref[0:1, :]
    cntp = cntp_ref[0:1, :]

    def bias_body(c, carry):
        bias_ref[c] = jnp.where(score_ref[c] >= tau, 0.0, NEG_INF)
        return carry

    lax.fori_loop(0, nk, bias_body, 0)

    @pl.when(jnp.max(cntp) > kf)
    def _():
        need = kf - count_where(lambda blk, thr: blk > thr, tau)
        r = lax.broadcasted_iota(jnp.int32, (TK, TK), 0)
        cc = lax.broadcasted_iota(jnp.int32, (TK, TK), 1)
        before = jnp.where(cc < r, 1.0, 0.0).astype(BF16)

        def tie_body(c, seen):
            sc = score_ref[c]
            eq = sc == tau
            eqf = jnp.where(eq, 1.0, 0.0)
            rank = jnp.dot(before, eqf.astype(BF16), preferred_element_type=F32) + seen
            keep = (sc > tau) | (eq & (rank < need))
            bias_ref[c] = jnp.where(keep, 0.0, NEG_INF)
            return seen + jnp.sum(eqf, axis=0, keepdims=True)

        lax.fori_loop(0, nk, tie_body, jnp.zeros((1, TQ), F32))

    m_ref[...] = jnp.full(m_ref.shape, NEG_INF, F32)
    l_ref[...] = jnp.zeros(l_ref.shape, F32)
    acc_ref[...] = jnp.zeros(acc_ref.shape, F32)
    alpha_ref[...] = jnp.ones(alpha_ref.shape, F32)
    p1_ref[...] = jnp.zeros(p1_ref.shape, BF16)
    ones_rows = jnp.ones((BF16_SUBLANES, TK), BF16)

    def logits_head(h, kc, bias, s_out, mc_out):
        cols = slice(h * TQ, (h + 1) * TQ)
        s = jnp.dot(kc, qpad_ref[:, cols], preferred_element_type=F32) + bias
        s_out[:, cols] = s
        mc_out[h] = jnp.broadcast_to(jnp.max(s, axis=0, keepdims=True), (SUBLANES, TQ))

    def pv_update(c, p_in):
        for g in range(N_KV_HEADS):
            lanes = slice(g * rep_heads * TQ, (g + 1) * rep_heads * TQ)
            lhs = jnp.concatenate([vt_ref[0, c, g * HEAD_DIM:(g + 1) * HEAD_DIM, :], ones_rows], axis=0)
            pv = jnp.dot(lhs, p_in[:, lanes], preferred_element_type=F32)
            for hh in range(rep_heads):
                h = g * rep_heads + hh
                rows = slice(h * HEAD_DIM, (h + 1) * HEAD_DIM)
                cols = slice(hh * TQ, (hh + 1) * TQ)
                alpha = alpha_ref[h, 0:1, :]
                acc_ref[rows, :] = alpha * acc_ref[rows, :] + pv[:HEAD_DIM, cols]
                l_ref[h] = jnp.broadcast_to(alpha * l_ref[h, 0:1, :] + pv[HEAD_DIM:HEAD_DIM + 1, cols],
                                            (SUBLANES, TQ))

    def softmax_head(h, s_in, mc_in, p_out):
        m_old = m_ref[h, 0:1, :]
        m_new = jnp.maximum(m_old, mc_in[h, 0:1, :])
        m_use = jnp.where(m_new == NEG_INF, 0.0, m_new)
        alpha_ref[h] = jnp.broadcast_to(jnp.exp2(m_old - m_use), (SUBLANES, TQ))
        m_ref[h] = jnp.broadcast_to(m_new, (SUBLANES, TQ))
        cols = slice(h * TQ, (h + 1) * TQ)
        for blk in range(TK // EXP_ROWS):
            rows = slice(blk * EXP_ROWS, (blk + 1) * EXP_ROWS)
            p_out[rows, cols] = jnp.exp2(s_in[rows, cols] - m_use).astype(BF16)

    def stage(c, s_cur, mc_cur, p_cur, s_nxt, mc_nxt, p_prev):
        c_nxt = jnp.minimum(c + 1, nk - 1)
        pv_update(jnp.maximum(c - 1, 0), p_prev)
        bias_nxt = bias_ref[c_nxt]
        k_nxt = k_ref[0, c_nxt]
        for h in range(N_HEADS):
            logits_head(h, k_nxt, bias_nxt, s_nxt, mc_nxt)
            softmax_head(h, s_cur, mc_cur, p_cur)

    for h in range(N_HEADS):
        logits_head(h, k_ref[0, 0], bias_ref[0], s0_ref, mc0_ref)

    def pair_body(j, carry):
        c = 2 * j
        stage(c, s0_ref, mc0_ref, p0_ref, s1_ref, mc1_ref, p1_ref)

        @pl.when(c + 1 < nk)
        def _():
            stage(c + 1, s1_ref, mc1_ref, p1_ref, s0_ref, mc0_ref, p0_ref)

        return carry

    lax.fori_loop(0, (nk + 1) // 2, pair_body, 0)

    @pl.when((nk - 1) % 2 == 0)
    def _():
        pv_update(nk - 1, p0_ref)

    @pl.when((nk - 1) % 2 == 1)
    def _():
        pv_update(nk - 1, p1_ref)

    for h in range(N_HEADS):
        rows = slice(h * HEAD_DIM, (h + 1) * HEAD_DIM)
        acc_ref[rows, :] = acc_ref[rows, :] / l_ref[h, 0:1, :]
    yb_ref[0] = acc_ref[...].T.astype(BF16)


def _run_attn(qi, wi, ki, q, k, vt, k_sel):
    b, s, _ = q.shape
    nc = s // TK
    ki = ki.reshape(b, nc, TK, IDX_DIM)
    k = k.reshape(b, nc, TK, D_KV)
    rows = lambda width: pl.BlockSpec((1, TQ, width), lambda bi, i: (bi, i, 0))
    keys = lambda d0, d1: pl.BlockSpec((1, nc, d0, d1), lambda bi, i: (bi, 0, 0, 0))
    return pl.pallas_call(
        functools.partial(_attn_kernel, k_sel=k_sel),
        grid=(b, s // TQ),
        in_specs=[rows(IDX_HEADS * IDX_DIM), rows(LANES), keys(TK, IDX_DIM),
                  rows(D_ATTN), keys(TK, D_KV), keys(D_KV, TK)],
        out_specs=rows(D_ATTN),
        out_shape=jax.ShapeDtypeStruct((b, s, D_ATTN), BF16),
        scratch_shapes=[pltpu.VMEM((nc, TK, TQ), F32),
                        pltpu.VMEM((nc, TK, TQ), F32),
                        pltpu.VMEM((IDX_DIM, IDX_HEADS * TQ), BF16),
                        pltpu.VMEM((D_KV, N_HEADS * TQ), BF16),
                        pltpu.VMEM((TK, N_HEADS * TQ), F32),
                        pltpu.VMEM((TK, N_HEADS * TQ), F32),
                        pltpu.VMEM((TK, N_HEADS * TQ), BF16),
                        pltpu.VMEM((TK, N_HEADS * TQ), BF16),
                        pltpu.VMEM((N_HEADS, SUBLANES, TQ), F32),
                        pltpu.VMEM((N_HEADS, SUBLANES, TQ), F32),
                        pltpu.VMEM((N_HEADS, SUBLANES, TQ), F32),
                        pltpu.VMEM((N_HEADS, SUBLANES, TQ), F32),
                        pltpu.VMEM((N_HEADS, SUBLANES, TQ), F32),
                        pltpu.VMEM((D_ATTN, TQ), F32),
                        pltpu.VMEM((SUBLANES, TQ), F32),
                        pltpu.VMEM((SUBLANES, TQ), F32),
                        pltpu.VMEM((SUBLANES, TQ), F32)],
        compiler_params=pltpu.CompilerParams(
            dimension_semantics=("parallel", "parallel"), vmem_limit_bytes=VMEM_LIMIT),
        name="dsa_attention",
    )(qi, wi, ki, q, k, vt)


def _tail_kernel(x_ref, ya_ref, yb_ref, wg_ref, wbr_ref, wo_ref, wup_ref, wdn_ref,
                 g1_ref, b1_ref, g2_ref, b2_ref, out_ref, *, alpha):
    part = TM_TAIL // TAIL_PARTS
    parts = [slice(p * part, (p + 1) * part) for p in range(TAIL_PARTS)]

    def mix_pre_ln(rows):
        x = x_ref[rows, :]
        xb = x.astype(BF16)
        merged = jnp.zeros((part, D_MODEL), F32)
        for n, y_ref in enumerate((ya_ref, yb_ref)):
            gate = _dot_nt(xb, wg_ref[n * D_MODEL:(n + 1) * D_MODEL, :])
            branch = jnp.dot(y_ref[rows, :], wbr_ref[n], preferred_element_type=F32)
            merged = merged + jax.nn.sigmoid(gate) * branch
        return alpha * x + jnp.dot(merged.astype(BF16), wo_ref[...], preferred_element_type=F32)

    def mlp_pre_ln(h):
        hb = h.astype(BF16)
        ff = jnp.zeros((part, D_MODEL), F32)
        for f in range(D_FF // FF_CHUNK):
            up = jnp.dot(hb, wup_ref[:, f * FF_CHUNK:(f + 1) * FF_CHUNK], preferred_element_type=F32)
            act = jnp.square(jnp.maximum(up, 0.0)).astype(BF16)
            ff = ff + jnp.dot(act, wdn_ref[f * FF_CHUNK:(f + 1) * FF_CHUNK, :], preferred_element_type=F32)
        return alpha * h + ff

    ln1 = lambda v: _layer_norm_rows(v, g1_ref[...], b1_ref[...])
    ln2 = lambda v: _layer_norm_rows(v, g2_ref[...], b2_ref[...])
    hs = [ln1(mix_pre_ln(rows)) for rows in parts]
    for rows, h in zip(parts, hs):
        out_ref[rows, :] = ln2(mlp_pre_ln(h))


def _run_tail(x2, ya2, yb2, wg, wbr, wo, wup, wdn, g1, b1, g2, b2, alpha):
    n, d = x2.shape
    const = lambda *shape: pl.BlockSpec(shape, lambda r: (0,) * len(shape),
                                        pipeline_mode=pl.Buffered(1))
    rows = lambda width: pl.BlockSpec((TM_TAIL, width), lambda r: (r, 0))
    return pl.pallas_call(
        functools.partial(_tail_kernel, alpha=alpha),
        grid=(n // TM_TAIL,),
        in_specs=[rows(d), rows(D_CONV), rows(D_ATTN),
                  const(N_BRANCH * d, d), const(N_BRANCH, D_CONV, d), const(d, d),
                  const(d, D_FF), const(D_FF, d),
                  const(1, d), const(1, d), const(1, d), const(1, d)],
        out_specs=rows(d),
        out_shape=jax.ShapeDtypeStruct((n, d), F32),
        compiler_params=pltpu.CompilerParams(
            dimension_semantics=("parallel",), vmem_limit_bytes=VMEM_LIMIT),
        name="merge_mlp",
    )(x2, ya2, yb2, wg, wbr, wo, wup, wdn, g1, b1, g2, b2)


def kernel(x, w_in, conv_w, idx_k_norm_g, idx_k_norm_b, w_branch, w_o, ln1_g, ln1_b, w_up, w_down, ln2_g, ln2_b):
    b, s, d = x.shape
    depth = w_in.shape[0]
    assert d == D_MODEL and s % TM_PROJ == 0 and (b * s) % TM_TAIL == 0 and TQ == TK
    k_sel = min(TOPK_MAX, s // 4)
    alpha = (2 * depth) ** 0.25
    for l in range(depth):
        wt = jnp.swapaxes(w_in[l], 0, 1)
        wa = wt[:OFF_KI].astype(BF16)
        wkw = jnp.concatenate(
            [wt[OFF_KI:OFF_WI], jnp.zeros((LANES - IDX_DIM, d), F32),
             wt[OFF_WI:OFF_GATES], jnp.zeros((LANES - IDX_HEADS, d), F32)], axis=0).astype(BF16)
        wg = wt[OFF_GATES:].astype(BF16)
        pad = lambda a: jnp.pad(a.astype(F32), (0, LANES - IDX_DIM)).reshape(1, LANES)
        ya, q, k, vt, qi, ki, wi = _run_proj(x, wa, wkw, conv_w[l], pad(idx_k_norm_g[l]), pad(idx_k_norm_b[l]))
        yb = _run_attn(qi, wi, ki, q, k, vt, k_sel)
        row = lambda a: a.astype(F32).reshape(1, d)
        out = _run_tail(x.reshape(b * s, d), ya.reshape(b * s, D_CONV), yb.reshape(b * s, D_ATTN),
                        wg, w_branch[l].astype(BF16), w_o[l].astype(BF16),
                        w_up[l].astype(BF16), w_down[l].astype(BF16),
                        row(ln1_g[l]), row(ln1_b[l]), row(ln2_g[l]), row(ln2_b[l]), alpha)
        x = out.reshape(b, s, d)
    return x
```

```python
import functools

import jax
import jax.numpy as jnp
from jax import lax
from jax.experimental import pallas as pl
from jax.experimental.pallas import tpu as pltpu

D_MODEL = 1024
D_CONV = 512
CONV_WIDTH = 3
N_HEADS = 8
N_KV_HEADS = 2
HEAD_DIM = 64
D_ATTN = N_HEADS * HEAD_DIM
D_KV = N_KV_HEADS * HEAD_DIM
IDX_HEADS = 8
IDX_DIM = 64
TOPK_MAX = 256
N_BRANCH = 2
D_FF = 4 * D_MODEL
LN_EPS = 1e-5
IDX_SCALE = (IDX_DIM ** -0.5) * (IDX_HEADS ** -0.5)
ATTN_SCALE = HEAD_DIM ** -0.5
LOG2_E = 1.4426950408889634
Q_SCALE = ATTN_SCALE * LOG2_E

OFF_Q = 3 * D_CONV
OFF_K = OFF_Q + D_ATTN
OFF_V = OFF_K + D_KV
OFF_QI = OFF_V + D_KV
OFF_KI = OFF_QI + IDX_HEADS * IDX_DIM
OFF_WI = OFF_KI + IDX_DIM
OFF_GATES = OFF_WI + IDX_HEADS

LANES = 128
SUBLANES = 8
BF16_SUBLANES = 16
VMEM_LIMIT = 56 * 1024 * 1024

TM_PROJ = 1024
TQ = 256
TK = 256
TM_TAIL = 512
TAIL_PARTS = 2
FF_CHUNK = 1024
COUNT_CHAINS = 4
EXP_ROWS = 32

F32 = jnp.float32
BF16 = jnp.bfloat16
NEG_INF = float("-inf")
F32_MAX = float(jnp.finfo(jnp.float32).max)
INT_MIN = -2 ** 31
KEY_BITS = 32


def _dot_nt(a, b_t):
    return lax.dot_general(a, b_t, (((1,), (1,)), ((), ())), preferred_element_type=F32)


def _layer_norm_rows(x, g, b):
    mu = jnp.mean(x, axis=-1, keepdims=True)
    xc = x - mu
    var = jnp.mean(xc * xc, axis=-1, keepdims=True)
    return xc * lax.rsqrt(var + LN_EPS) * g + b


def _proj_kernel(x_ref, wa_ref, wkw_ref, convw_ref, kg_ref, kb_ref,
                 ya_ref, q_ref, k_ref, vt_ref, qi_ref, ki_ref, wi_ref,
                 carry_ref):
    @pl.when(pl.program_id(1) == 0)
    def _():
        carry_ref[...] = jnp.zeros_like(carry_ref)

    xb = x_ref[0].astype(BF16)

    def proj(lo, hi):
        return _dot_nt(xb, wa_ref[lo:hi, :])

    zkw = _dot_nt(xb, wkw_ref[...])
    wi_ref[0] = zkw[:, LANES:]
    zk = zkw[:, :LANES]
    valid = lax.broadcasted_iota(jnp.int32, zk.shape, 1) < IDX_DIM
    mu = jnp.sum(jnp.where(valid, zk, 0.0), axis=-1, keepdims=True) * (1.0 / IDX_DIM)
    xc = jnp.where(valid, zk - mu, 0.0)
    var = jnp.sum(xc * xc, axis=-1, keepdims=True) * (1.0 / IDX_DIM)
    kin = xc * lax.rsqrt(var + LN_EPS) * kg_ref[...] + kb_ref[...]
    ki_ref[0] = kin[:, :IDX_DIM].astype(BF16)

    vt = proj(OFF_V, OFF_QI).T
    for c in range(TM_PROJ // TK):
        vt_ref[0, c] = vt[:, c * TK:(c + 1) * TK].astype(BF16)

    cu = proj(D_CONV, 2 * D_CONV) * proj(2 * D_CONV, 3 * D_CONV)
    prev = carry_ref[...]
    row = lax.broadcasted_iota(jnp.int32, cu.shape, 0)
    cu1 = jnp.where(row == 0, prev[SUBLANES - 1:SUBLANES, :], pltpu.roll(cu, 1, axis=0))
    cu2 = jnp.where(row == 0, prev[SUBLANES - 2:SUBLANES - 1, :],
                    jnp.where(row == 1, prev[SUBLANES - 1:SUBLANES, :], pltpu.roll(cu, 2, axis=0)))
    carry_ref[...] = cu[TM_PROJ - SUBLANES:, :]
    cw = convw_ref[...]
    conv = cw[0:1, :] * cu2 + cw[1:2, :] * cu1 + cw[2:3, :] * cu
    ya_ref[0] = (proj(0, D_CONV) * conv).astype(BF16)

    k_ref[0] = proj(OFF_K, OFF_V).astype(BF16)
    q_ref[0] = (proj(OFF_Q, OFF_K) * Q_SCALE).astype(BF16)
    qi_ref[0] = proj(OFF_QI, OFF_KI).astype(BF16)


def _run_proj(x, wa, wkw, conv_w, kg, kb):
    b, s, d = x.shape
    nc = s // TK
    cpt = TM_PROJ // TK
    const = lambda *shape: pl.BlockSpec(shape, lambda bi, j: (0,) * len(shape),
                                        pipeline_mode=pl.Buffered(1))
    rows = lambda width: pl.BlockSpec((1, TM_PROJ, width), lambda bi, j: (bi, j, 0))
    return pl.pallas_call(
        _proj_kernel,
        grid=(b, s // TM_PROJ),
        in_specs=[rows(d), const(OFF_KI, d), const(2 * LANES, d), const(CONV_WIDTH, D_CONV),
                  const(1, LANES), const(1, LANES)],
        out_specs=[rows(D_CONV), rows(D_ATTN), rows(D_KV),
                   pl.BlockSpec((1, cpt, D_KV, TK), lambda bi, j: (bi, j, 0, 0)),
                   rows(IDX_HEADS * IDX_DIM), rows(IDX_DIM), rows(LANES)],
        out_shape=[jax.ShapeDtypeStruct((b, s, D_CONV), BF16),
                   jax.ShapeDtypeStruct((b, s, D_ATTN), BF16),
                   jax.ShapeDtypeStruct((b, s, D_KV), BF16),
                   jax.ShapeDtypeStruct((b, nc, D_KV, TK), BF16),
                   jax.ShapeDtypeStruct((b, s, IDX_HEADS * IDX_DIM), BF16),
                   jax.ShapeDtypeStruct((b, s, IDX_DIM), BF16),
                   jax.ShapeDtypeStruct((b, s, LANES), F32)],
        scratch_shapes=[pltpu.VMEM((SUBLANES, D_CONV), F32)],
        compiler_params=pltpu.CompilerParams(
            dimension_semantics=("parallel", "arbitrary"), vmem_limit_bytes=VMEM_LIMIT),
        name="proj_mixer_a",
    )(x, wa, wkw, conv_w, kg, kb)


def _sortable_to_f32(x):
    bits = x ^ ((x >> 31) & jnp.int32(0x7FFFFFFF))
    return pltpu.bitcast(bits, F32)


def _attn_kernel(qi_ref, wi_ref, ki_ref, q_ref, k_ref, vt_ref, yb_ref,
                 score_ref, bias_ref, qit_ref, qpad_ref, s0_ref, s1_ref, p0_ref, p1_ref,
                 mc0_ref, mc1_ref, alpha_ref, m_ref, l_ref, acc_ref, thr_ref, tau_ref, cntp_ref, *, k_sel):
    i = pl.program_id(1)
    nk = i + 1
    kf = float(k_sel)
    rep_heads = N_HEADS // N_KV_HEADS
    n_groups = TK // SUBLANES
    n_lane_groups = TQ // LANES

    qit = qi_ref[0].astype(F32).T.astype(BF16)
    for h in range(IDX_HEADS):
        qit_ref[:, h * TQ:(h + 1) * TQ] = qit[h * IDX_DIM:(h + 1) * IDX_DIM, :]
    wt = wi_ref[0].T[:IDX_HEADS, :]
    qt = q_ref[0].astype(F32).T.astype(BF16)
    zero_half = jnp.zeros((HEAD_DIM, TQ), BF16)
    for h in range(N_HEADS):
        g = h // rep_heads
        for gg in range(N_KV_HEADS):
            qpad_ref[gg * HEAD_DIM:(gg + 1) * HEAD_DIM, h * TQ:(h + 1) * TQ] = (
                qt[h * HEAD_DIM:(h + 1) * HEAD_DIM, :] if gg == g else zero_half)

    row = lax.broadcasted_iota(jnp.int32, (TK, TQ), 0)
    col = lax.broadcasted_iota(jnp.int32, (TK, TQ), 1)
    row_minus_col = row - col

    def score_chunk(c):
        kic = ki_ref[0, c]
        lg_all = jnp.dot(kic, qit_ref[...], preferred_element_type=F32)
        acc = jnp.zeros((TK, TQ), F32)
        for h in range(IDX_HEADS):
            lg = lg_all[:, h * TQ:(h + 1) * TQ]
            acc = acc + wt[h:h + 1, :] * jnp.maximum(lg, 0.0)
        noncausal = row_minus_col > (i - c) * TQ
        score_ref[c] = jnp.where(noncausal, NEG_INF, acc * IDX_SCALE)

    def score_pair(j, carry):
        score_chunk(2 * j)
        score_chunk(2 * j + 1)
        return carry

    lax.fori_loop(0, nk // 2, score_pair, 0)

    @pl.when(nk % 2 == 1)
    def _():
        score_chunk(nk - 1)

    def count_where(pred, thr):
        thr_ref[...] = jnp.broadcast_to(thr, (SUBLANES, TQ))
        thr_g = [thr_ref[:, g * LANES:(g + 1) * LANES] for g in range(n_lane_groups)]

        def chunk_counts(c, accs, diagonal):
            accs = [list(a) for a in accs]
            for r in range(n_groups):
                for g in range(n_lane_groups):
                    if diagonal and r * SUBLANES >= (g + 1) * LANES:
                        continue
                    blk = score_ref[c, r * SUBLANES:(r + 1) * SUBLANES, g * LANES:(g + 1) * LANES]
                    a = accs[g][r % COUNT_CHAINS]
                    accs[g][r % COUNT_CHAINS] = jnp.where(pred(blk, thr_g[g]), a + 1.0, a)
            return tuple(tuple(a) for a in accs)

        zero = jnp.zeros((SUBLANES, LANES), F32)
        accs = ((zero,) * COUNT_CHAINS,) * n_lane_groups
        accs = lax.fori_loop(0, nk - 1, lambda c, a: chunk_counts(c, a, False), accs)
        accs = chunk_counts(nk - 1, accs, True)
        per_group = [sum(a[1:], a[0]) for a in accs]
        return jnp.sum(jnp.concatenate(per_group, axis=1), axis=0, keepdims=True)

    def count_ge(cand_key):
        return count_where(lambda blk, thr: blk >= thr, _sortable_to_f32(cand_key))

    def bit_body(b, carry):
        prefix, cntp = carry
        cand_key = prefix | jnp.left_shift(jnp.int32(1), KEY_BITS - 2 - b)
        cnt = count_ge(cand_key)
        ok = cnt >= kf
        return jnp.where(ok, cand_key, prefix), jnp.where(ok, cnt, cntp)

    tau_ref[...] = jnp.full(tau_ref.shape, -F32_MAX, F32)
    cntp_ref[...] = jnp.zeros(cntp_ref.shape, F32)

    @pl.when(nk * TK > k_sel)
    def _():
        cnt0 = count_ge(jnp.zeros((1, TQ), jnp.int32))
        ok0 = cnt0 >= kf
        prefix0 = jnp.where(ok0, jnp.int32(0), jnp.int32(INT_MIN))
        prefix, cntp = lax.fori_loop(0, KEY_BITS - 1, bit_body, (prefix0, jnp.where(ok0, cnt0, 0.0)))
        tau_found = jnp.where(prefix == INT_MIN, -F32_MAX, _sortable_to_f32(prefix))
        tau_ref[...] = jnp.broadcast_to(tau_found, tau_ref.shape)
        cntp_ref[...] = jnp.broadcast_to(cntp, cntp_ref.shape)

    tau = tau_ref[0:1, :]
    cntp = cntp_ref[0:1, :]

    def bias_body(c, carry):
        bias_ref[c] = jnp.where(score_ref[c] >= tau, 0.0, NEG_INF)
        return carry

    lax.fori_loop(0, nk, bias_body, 0)

    @pl.when(jnp.max(cntp) > kf)
    def _():
        need = kf - count_where(lambda blk, thr: blk > thr, tau)
        r = lax.broadcasted_iota(jnp.int32, (TK, TK), 0)
        cc = lax.broadcasted_iota(jnp.int32, (TK, TK), 1)
        before = jnp.where(cc < r, 1.0, 0.0).astype(BF16)

        def tie_body(c, seen):
            sc = score_ref[c]
            eq = sc == tau
            eqf = jnp.where(eq, 1.0, 0.0)
            rank = jnp.dot(before, eqf.astype(BF16), preferred_element_type=F32) + seen
            keep = (sc > tau) | (eq & (rank < need))
            bias_ref[c] = jnp.where(keep, 0.0, NEG_INF)
            return seen + jnp.sum(eqf, axis=0, keepdims=True)

        lax.fori_loop(0, nk, tie_body, jnp.zeros((1, TQ), F32))

    m_ref[...] = jnp.full(m_ref.shape, NEG_INF, F32)
    l_ref[...] = jnp.zeros(l_ref.shape, F32)
    acc_ref[...] = jnp.zeros(acc_ref.shape, F32)
    alpha_ref[...] = jnp.ones(alpha_ref.shape, F32)
    p1_ref[...] = jnp.zeros(p1_ref.shape, BF16)
    ones_rows = jnp.ones((BF16_SUBLANES, TK), BF16)

    def logits_head(h, kc, bias, s_out, mc_out):
        cols = slice(h * TQ, (h + 1) * TQ)
        s = jnp.dot(kc, qpad_ref[:, cols], preferred_element_type=F32) + bias
        s_out[:, cols] = s
        mc_out[h] = jnp.broadcast_to(jnp.max(s, axis=0, keepdims=True), (SUBLANES, TQ))

    def pv_update(c, p_in):
        for g in range(N_KV_HEADS):
            lanes = slice(g * rep_heads * TQ, (g + 1) * rep_heads * TQ)
            lhs = jnp.concatenate([vt_ref[0, c, g * HEAD_DIM:(g + 1) * HEAD_DIM, :], ones_rows], axis=0)
            pv = jnp.dot(lhs, p_in[:, lanes], preferred_element_type=F32)
            for hh in range(rep_heads):
                h = g * rep_heads + hh
                rows = slice(h * HEAD_DIM, (h + 1) * HEAD_DIM)
                cols = slice(hh * TQ, (hh + 1) * TQ)
                alpha = alpha_ref[h, 0:1, :]
                acc_ref[rows, :] = alpha * acc_ref[rows, :] + pv[:HEAD_DIM, cols]
                l_ref[h] = jnp.broadcast_to(alpha * l_ref[h, 0:1, :] + pv[HEAD_DIM:HEAD_DIM + 1, cols],
                                            (SUBLANES, TQ))

    def softmax_head(h, s_in, mc_in, p_out):
        m_old = m_ref[h, 0:1, :]
        m_new = jnp.maximum(m_old, mc_in[h, 0:1, :])
        m_use = jnp.where(m_new == NEG_INF, 0.0, m_new)
        alpha_ref[h] = jnp.broadcast_to(jnp.exp2(m_old - m_use), (SUBLANES, TQ))
        m_ref[h] = jnp.broadcast_to(m_new, (SUBLANES, TQ))
        cols = slice(h * TQ, (h + 1) * TQ)
        for blk in range(TK // EXP_ROWS):
            rows = slice(blk * EXP_ROWS, (blk + 1) * EXP_ROWS)
            p_out[rows, cols] = jnp.exp2(s_in[rows, cols] - m_use).astype(BF16)

    def stage(c, s_cur, mc_cur, p_cur, s_nxt, mc_nxt, p_prev):
        c_nxt = jnp.minimum(c + 1, nk - 1)
        pv_update(jnp.maximum(c - 1, 0), p_prev)
        bias_nxt = bias_ref[c_nxt]
        k_nxt = k_ref[0, c_nxt]
        for h in range(N_HEADS):
            logits_head(h, k_nxt, bias_nxt, s_nxt, mc_nxt)
            softmax_head(h, s_cur, mc_cur, p_cur)

    for h in range(N_HEADS):
        logits_head(h, k_ref[0, 0], bias_ref[0], s0_ref, mc0_ref)

    def pair_body(j, carry):
        c = 2 * j
        stage(c, s0_ref, mc0_ref, p0_ref, s1_ref, mc1_ref, p1_ref)

        @pl.when(c + 1 < nk)
        def _():
            stage(c + 1, s1_ref, mc1_ref, p1_ref, s0_ref, mc0_ref, p0_ref)

        return carry

    lax.fori_loop(0, (nk + 1) // 2, pair_body, 0)

    @pl.when((nk - 1) % 2 == 0)
    def _():
        pv_update(nk - 1, p0_ref)

    @pl.when((nk - 1) % 2 == 1)
    def _():
        pv_update(nk - 1, p1_ref)

    for h in range(N_HEADS):
        rows = slice(h * HEAD_DIM, (h + 1) * HEAD_DIM)
        acc_ref[rows, :] = acc_ref[rows, :] / l_ref[h, 0:1, :]
    yb_ref[0] = acc_ref[...].T.astype(BF16)


def _run_attn(qi, wi, ki, q, k, vt, k_sel):
    b, s, _ = q.shape
    nc = s // TK
    ki = ki.reshape(b, nc, TK, IDX_DIM)
    k = k.reshape(b, nc, TK, D_KV)
    rows = lambda width: pl.BlockSpec((1, TQ, width), lambda bi, i: (bi, i, 0))
    keys = lambda d0, d1: pl.BlockSpec((1, nc, d0, d1), lambda bi, i: (bi, 0, 0, 0))
    return pl.pallas_call(
        functools.partial(_attn_kernel, k_sel=k_sel),
        grid=(b, s // TQ),
        in_specs=[rows(IDX_HEADS * IDX_DIM), rows(LANES), keys(TK, IDX_DIM),
                  rows(D_ATTN), keys(TK, D_KV), keys(D_KV, TK)],
        out_specs=rows(D_ATTN),
        out_shape=jax.ShapeDtypeStruct((b, s, D_ATTN), BF16),
        scratch_shapes=[pltpu.VMEM((nc, TK, TQ), F32),
                        pltpu.VMEM((nc, TK, TQ), F32),
                        pltpu.VMEM((IDX_DIM, IDX_HEADS * TQ), BF16),
                        pltpu.VMEM((D_KV, N_HEADS * TQ), BF16),
                        pltpu.VMEM((TK, N_HEADS * TQ), F32),
                        pltpu.VMEM((TK, N_HEADS * TQ), F32),
                        pltpu.VMEM((TK, N_HEADS * TQ), BF16),
                        pltpu.VMEM((TK, N_HEADS * TQ), BF16),
                        pltpu.VMEM((N_HEADS, SUBLANES, TQ), F32),
                        pltpu.VMEM((N_HEADS, SUBLANES, TQ), F32),
                        pltpu.VMEM((N_HEADS, SUBLANES, TQ), F32),
                        pltpu.VMEM((N_HEADS, SUBLANES, TQ), F32),
                        pltpu.VMEM((N_HEADS, SUBLANES, TQ), F32),
                        pltpu.VMEM((D_ATTN, TQ), F32),
                        pltpu.VMEM((SUBLANES, TQ), F32),
                        pltpu.VMEM((SUBLANES, TQ), F32),
                        pltpu.VMEM((SUBLANES, TQ), F32)],
        compiler_params=pltpu.CompilerParams(
            dimension_semantics=("parallel", "parallel"), vmem_limit_bytes=VMEM_LIMIT),
        name="dsa_attention",
    )(qi, wi, ki, q, k, vt)


def _tail_kernel(x_ref, ya_ref, yb_ref, wg_ref, wbr_ref, wo_ref, wup_ref, wdn_ref,
                 g1_ref, b1_ref, g2_ref, b2_ref, out_ref, *, alpha):
    part = TM_TAIL // TAIL_PARTS
    parts = [slice(p * part, (p + 1) * part) for p in range(TAIL_PARTS)]

    def mix_pre_ln(rows):
        x = x_ref[rows, :]
        xb = x.astype(BF16)
        merged = jnp.zeros((part, D_MODEL), F32)
        for n, y_ref in enumerate((ya_ref, yb_ref)):
            gate = _dot_nt(xb, wg_ref[n * D_MODEL:(n + 1) * D_MODEL, :])
            branch = jnp.dot(y_ref[rows, :], wbr_ref[n], preferred_element_type=F32)
            merged = merged + jax.nn.sigmoid(gate) * branch
        return alpha * x + jnp.dot(merged.astype(BF16), wo_ref[...], preferred_element_type=F32)

    def mlp_pre_ln(h):
        hb = h.astype(BF16)
        ff = jnp.zeros((part, D_MODEL), F32)
        for f in range(D_FF // FF_CHUNK):
            up = jnp.dot(hb, wup_ref[:, f * FF_CHUNK:(f + 1) * FF_CHUNK], preferred_element_type=F32)
            act = jnp.square(jnp.maximum(up, 0.0)).astype(BF16)
            ff = ff + jnp.dot(act, wdn_ref[f * FF_CHUNK:(f + 1) * FF_CHUNK, :], preferred_element_type=F32)
        return alpha * h + ff

    ln1 = lambda v: _layer_norm_rows(v, g1_ref[...], b1_ref[...])
    ln2 = lambda v: _layer_norm_rows(v, g2_ref[...], b2_ref[...])
    hs = [ln1(mix_pre_ln(rows)) for rows in parts]
    for rows, h in zip(parts, hs):
        out_ref[rows, :] = ln2(mlp_pre_ln(h))


def _run_tail(x2, ya2, yb2, wg, wbr, wo, wup, wdn, g1, b1, g2, b2, alpha):
    n, d = x2.shape
    const = lambda *shape: pl.BlockSpec(shape, lambda r: (0,) * len(shape),
                                        pipeline_mode=pl.Buffered(1))
    rows = lambda width: pl.BlockSpec((TM_TAIL, width), lambda r: (r, 0))
    return pl.pallas_call(
        functools.partial(_tail_kernel, alpha=alpha),
        grid=(n // TM_TAIL,),
        in_specs=[rows(d), rows(D_CONV), rows(D_ATTN),
                  const(N_BRANCH * d, d), const(N_BRANCH, D_CONV, d), const(d, d),
                  const(d, D_FF), const(D_FF, d),
                  const(1, d), const(1, d), const(1, d), const(1, d)],
        out_specs=rows(d),
        out_shape=jax.ShapeDtypeStruct((n, d), F32),
        compiler_params=pltpu.CompilerParams(
            dimension_semantics=("parallel",), vmem_limit_bytes=VMEM_LIMIT),
        name="merge_mlp",
    )(x2, ya2, yb2, wg, wbr, wo, wup, wdn, g1, b1, g2, b2)


def kernel(x, w_in, conv_w, idx_k_norm_g, idx_k_norm_b, w_branch, w_o, ln1_g, ln1_b, w_up, w_down, ln2_g, ln2_b):
    b, s, d = x.shape
    depth = w_in.shape[0]
    assert d == D_MODEL and s % TM_PROJ == 0 and (b * s) % TM_TAIL == 0 and TQ == TK
    k_sel = min(TOPK_MAX, s // 4)
    alpha = (2 * depth) ** 0.25
    for l in range(depth):
        wt = jnp.swapaxes(w_in[l], 0, 1)
        wa = wt[:OFF_KI].astype(BF16)
        wkw = jnp.concatenate(
            [wt[OFF_KI:OFF_WI], jnp.zeros((LANES - IDX_DIM, d), F32),
             wt[OFF_WI:OFF_GATES], jnp.zeros((LANES - IDX_HEADS, d), F32)], axis=0).astype(BF16)
        wg = wt[OFF_GATES:].astype(BF16)
        pad = lambda a: jnp.pad(a.astype(F32), (0, LANES - IDX_DIM)).reshape(1, LANES)
        ya, q, k, vt, qi, ki, wi = _run_proj(x, wa, wkw, conv_w[l], pad(idx_k_norm_g[l]), pad(idx_k_norm_b[l]))
        yb = _run_attn(qi, wi, ki, q, k, vt, k_sel)
        row = lambda a: a.astype(F32).reshape(1, d)
        out = _run_tail(x.reshape(b * s, d), ya.reshape(b * s, D_CONV), yb.reshape(b * s, D_ATTN),
                        wg, w_branch[l].astype(BF16), w_o[l].astype(BF16),
                        w_up[l].astype(BF16), w_down[l].astype(BF16),
                        row(ln1_g[l]), row(ln1_b[l]), row(ln2_g[l]), row(ln2_b[l]), alpha)
        x = out.reshape(b, s, d)
    return x
```

```python
import functools

import jax
import jax.numpy as jnp
from jax import lax
from jax.experimental import pallas as pl
from jax.experimental.pallas import tpu as pltpu

D_MODEL = 1024
D_CONV = 512
CONV_WIDTH = 3
N_HEADS = 8
N_KV_HEADS = 2
HEAD_DIM = 64
D_ATTN = N_HEADS * HEAD_DIM
D_KV = N_KV_HEADS * HEAD_DIM
IDX_HEADS = 8
IDX_DIM = 64
TOPK_MAX = 256
N_BRANCH = 2
D_FF = 4 * D_MODEL
LN_EPS = 1e-5
IDX_SCALE = (IDX_DIM ** -0.5) * (IDX_HEADS ** -0.5)
ATTN_SCALE = HEAD_DIM ** -0.5
LOG2_E = 1.4426950408889634
Q_SCALE = ATTN_SCALE * LOG2_E

OFF_Q = 3 * D_CONV
OFF_K = OFF_Q + D_ATTN
OFF_V = OFF_K + D_KV
OFF_QI = OFF_V + D_KV
OFF_KI = OFF_QI + IDX_HEADS * IDX_DIM
OFF_WI = OFF_KI + IDX_DIM
OFF_GATES = OFF_WI + IDX_HEADS

LANES = 128
SUBLANES = 8
BF16_SUBLANES = 16
VMEM_LIMIT = 56 * 1024 * 1024

TM_PROJ = 1024
TQ = 256
TK = 256
TM_TAIL = 512
TAIL_PARTS = 2
FF_CHUNK = 1024
COUNT_CHAINS = 4
EXP_ROWS = 32

F32 = jnp.float32
BF16 = jnp.bfloat16
NEG_INF = float("-inf")
F32_MAX = float(jnp.finfo(jnp.float32).max)
INT_MIN = -2 ** 31
KEY_BITS = 32
CAST_ROWS = (N_BRANCH * D_CONV, D_MODEL, D_MODEL, D_FF)


def _dot_nt(a, b_t):
    return lax.dot_general(a, b_t, (((1,), (1,)), ((), ())), preferred_element_type=F32)


def _layer_norm_rows(x, g, b):
    mu = jnp.mean(x, axis=-1, keepdims=True)
    xc = x - mu
    var = jnp.mean(xc * xc, axis=-1, keepdims=True)
    return xc * lax.rsqrt(var + LN_EPS) * g + b


def _proj_kernel(x_ref, wa_ref, wkw_ref, convw_ref, kg_ref, kb_ref, *rest):
    n_cast = len(CAST_ROWS)
    cast_in, rest = rest[:n_cast], rest[n_cast:]
    ya_ref, q_ref, k_ref, vt_ref, qi_ref, ki_ref, wi_ref = rest[:7]
    cast_out, carry_ref = rest[7:7 + n_cast], rest[7 + n_cast]
    for src, dst in zip(cast_in, cast_out):
        dst[...] = src[...].astype(BF16)

    @pl.when(pl.program_id(1) == 0)
    def _():
        carry_ref[...] = jnp.zeros_like(carry_ref)

    xb = x_ref[0].astype(BF16)

    def proj(lo, hi):
        return _dot_nt(xb, wa_ref[lo:hi, :])

    zkw = _dot_nt(xb, wkw_ref[...])
    wi_ref[0] = zkw[:, LANES:]
    zk = zkw[:, :LANES]
    valid = lax.broadcasted_iota(jnp.int32, zk.shape, 1) < IDX_DIM
    mu = jnp.sum(jnp.where(valid, zk, 0.0), axis=-1, keepdims=True) * (1.0 / IDX_DIM)
    xc = jnp.where(valid, zk - mu, 0.0)
    var = jnp.sum(xc * xc, axis=-1, keepdims=True) * (1.0 / IDX_DIM)
    kin = xc * lax.rsqrt(var + LN_EPS) * kg_ref[...] + kb_ref[...]
    ki_ref[0] = kin[:, :IDX_DIM].astype(BF16)

    vt = proj(OFF_V, OFF_QI).T
    for c in range(TM_PROJ // TK):
        vt_ref[0, c] = vt[:, c * TK:(c + 1) * TK].astype(BF16)

    cu = proj(D_CONV, 2 * D_CONV) * proj(2 * D_CONV, 3 * D_CONV)
    prev = carry_ref[...]
    row = lax.broadcasted_iota(jnp.int32, cu.shape, 0)
    cu1 = jnp.where(row == 0, prev[SUBLANES - 1:SUBLANES, :], pltpu.roll(cu, 1, axis=0))
    cu2 = jnp.where(row == 0, prev[SUBLANES - 2:SUBLANES - 1, :],
                    jnp.where(row == 1, prev[SUBLANES - 1:SUBLANES, :], pltpu.roll(cu, 2, axis=0)))
    carry_ref[...] = cu[TM_PROJ - SUBLANES:, :]
    cw = convw_ref[...]
    conv = cw[0:1, :] * cu2 + cw[1:2, :] * cu1 + cw[2:3, :] * cu
    ya_ref[0] = (proj(0, D_CONV) * conv).astype(BF16)

    k_ref[0] = proj(OFF_K, OFF_V).astype(BF16)
    q_ref[0] = (proj(OFF_Q, OFF_K) * Q_SCALE).astype(BF16)
    qi_ref[0] = proj(OFF_QI, OFF_KI).astype(BF16)


def _run_proj(x, wa, wkw, conv_w, kg, kb, cast_weights):
    b, s, d = x.shape
    nc = s // TK
    cpt = TM_PROJ // TK
    steps = s // TM_PROJ
    n_steps = b * steps
    assert tuple(w.shape[0] for w in cast_weights) == CAST_ROWS
    cast_specs = [pl.BlockSpec((w.shape[0] // n_steps, w.shape[1]), lambda bi, j: (bi * steps + j, 0))
                  for w in cast_weights]
    const = lambda *shape: pl.BlockSpec(shape, lambda bi, j: (0,) * len(shape),
                                        pipeline_mode=pl.Buffered(1))
    rows = lambda width: pl.BlockSpec((1, TM_PROJ, width), lambda bi, j: (bi, j, 0))
    return pl.pallas_call(
        _proj_kernel,
        grid=(b, s // TM_PROJ),
        in_specs=[rows(d), const(OFF_KI, d), const(2 * LANES, d), const(CONV_WIDTH, D_CONV),
                  const(1, LANES), const(1, LANES)] + cast_specs,
        out_specs=[rows(D_CONV), rows(D_ATTN), rows(D_KV),
                   pl.BlockSpec((1, cpt, D_KV, TK), lambda bi, j: (bi, j, 0, 0)),
                   rows(IDX_HEADS * IDX_DIM), rows(IDX_DIM), rows(LANES)] + cast_specs,
        out_shape=[jax.ShapeDtypeStruct((b, s, D_CONV), BF16),
                   jax.ShapeDtypeStruct((b, s, D_ATTN), BF16),
                   jax.ShapeDtypeStruct((b, s, D_KV), BF16),
                   jax.ShapeDtypeStruct((b, nc, D_KV, TK), BF16),
                   jax.ShapeDtypeStruct((b, s, IDX_HEADS * IDX_DIM), BF16),
                   jax.ShapeDtypeStruct((b, s, IDX_DIM), BF16),
                   jax.ShapeDtypeStruct((b, s, LANES), F32)]
        + [jax.ShapeDtypeStruct(w.shape, BF16) for w in cast_weights],
        scratch_shapes=[pltpu.VMEM((SUBLANES, D_CONV), F32)],
        compiler_params=pltpu.CompilerParams(
            dimension_semantics=("parallel", "arbitrary"), vmem_limit_bytes=VMEM_LIMIT),
        name="proj_mixer_a",
    )(x, wa, wkw, conv_w, kg, kb, *cast_weights)


def _sortable_to_f32(x):
    bits = x ^ ((x >> 31) & jnp.int32(0x7FFFFFFF))
    return pltpu.bitcast(bits, F32)


def _attn_kernel(qi_ref, wi_ref, ki_ref, q_ref, k_ref, vt_ref, yb_ref,
                 score_ref, bias_ref, qit_ref, qpad_ref, s0_ref, s1_ref, p0_ref, p1_ref,
                 mc0_ref, mc1_ref, alpha_ref, m_ref, l_ref, acc_ref, thr_ref, tau_ref, cntp_ref, *, k_sel):
    i = pl.program_id(1)
    nk = i + 1
    kf = float(k_sel)
    rep_heads = N_HEADS // N_KV_HEADS
    n_groups = TK // SUBLANES
    n_lane_groups = TQ // LANES

    qit = qi_ref[0].astype(F32).T.astype(BF16)
    for h in range(IDX_HEADS):
        qit_ref[:, h * TQ:(h + 1) * TQ] = qit[h * IDX_DIM:(h + 1) * IDX_DIM, :]
    wt = wi_ref[0].T[:IDX_HEADS, :]
    qt = q_ref[0].astype(F32).T.astype(BF16)
    zero_half = jnp.zeros((HEAD_DIM, TQ), BF16)
    for h in range(N_HEADS):
        g = h // rep_heads
        for gg in range(N_KV_HEADS):
            qpad_ref[gg * HEAD_DIM:(gg + 1) * HEAD_DIM, h * TQ:(h + 1) * TQ] = (
                qt[h * HEAD_DIM:(h + 1) * HEAD_DIM, :] if gg == g else zero_half)

    row = lax.broadcasted_iota(jnp.int32, (TK, TQ), 0)
    col = lax.broadcasted_iota(jnp.int32, (TK, TQ), 1)
    row_minus_col = row - col

    def score_chunk(c):
        kic = ki_ref[0, c]
        lg_all = jnp.dot(kic, qit_ref[...], preferred_element_type=F32)
        acc = jnp.zeros((TK, TQ), F32)
        for h in range(IDX_HEADS):
            lg = lg_all[:, h * TQ:(h + 1) * TQ]
            acc = acc + wt[h:h + 1, :] * jnp.maximum(lg, 0.0)
        noncausal = row_minus_col > (i - c) * TQ
        score_ref[c] = jnp.where(noncausal, NEG_INF, acc * IDX_SCALE)

    def score_pair(j, carry):
        score_chunk(2 * j)
        score_chunk(2 * j + 1)
        return carry

    lax.fori_loop(0, nk // 2, score_pair, 0)

    @pl.when(nk % 2 == 1)
    def _():
        score_chunk(nk - 1)

    def count_where(pred, thr):
        thr_ref[...] = jnp.broadcast_to(thr, (SUBLANES, TQ))
        thr_g = [thr_ref[:, g * LANES:(g + 1) * LANES] for g in range(n_lane_groups)]

        def chunk_counts(c, accs, diagonal):
            accs = [list(a) for a in accs]
            for r in range(n_groups):
                for g in range(n_lane_groups):
                    if diagonal and r * SUBLANES >= (g + 1) * LANES:
                        continue
                    blk = score_ref[c, r * SUBLANES:(r + 1) * SUBLANES, g * LANES:(g + 1) * LANES]
                    a = accs[g][r % COUNT_CHAINS]
                    accs[g][r % COUNT_CHAINS] = jnp.where(pred(blk, thr_g[g]), a + 1.0, a)
            return tuple(tuple(a) for a in accs)

        zero = jnp.zeros((SUBLANES, LANES), F32)
        accs = ((zero,) * COUNT_CHAINS,) * n_lane_groups
        accs = lax.fori_loop(0, nk - 1, lambda c, a: chunk_counts(c, a, False), accs)
        accs = chunk_counts(nk - 1, accs, True)
        per_group = [sum(a[1:], a[0]) for a in accs]
        return jnp.sum(jnp.concatenate(per_group, axis=1), axis=0, keepdims=True)

    def count_ge(cand_key):
        return count_where(lambda blk, thr: blk >= thr, _sortable_to_f32(cand_key))

    def bit_body(b, carry):
        prefix, cntp = carry
        cand_key = prefix | jnp.left_shift(jnp.int32(1), KEY_BITS - 2 - b)
        cnt = count_ge(cand_key)
        ok = cnt >= kf
        return jnp.where(ok, cand_key, prefix), jnp.where(ok, cnt, cntp)

    tau_ref[...] = jnp.full(tau_ref.shape, -F32_MAX, F32)
    cntp_ref[...] = jnp.zeros(cntp_ref.shape, F32)

    @pl.when(nk * TK > k_sel)
    def _():
        cnt0 = count_ge(jnp.zeros((1, TQ), jnp.int32))
        ok0 = cnt0 >= kf
        prefix0 = jnp.where(ok0, jnp.int32(0), jnp.int32(INT_MIN))
        prefix, cntp = lax.fori_loop(0, KEY_BITS - 1, bit_body, (prefix0, jnp.where(ok0, cnt0, 0.0)))
        tau_found = jnp.where(prefix == INT_MIN, -F32_MAX, _sortable_to_f32(prefix))
        tau_ref[...] = jnp.broadcast_to(tau_found, tau_ref.shape)
        cntp_ref[...] = jnp.broadcast_to(cntp, cntp_ref.shape)

    tau = tau_ref[0:1, :]
    cntp = cntp_ref[0:1, :]

    def bias_body(c, carry):
        bias_ref[c] = jnp.where(score_ref[c] >= tau, 0.0, NEG_INF)
        return carry

    lax.fori_loop(0, nk, bias_body, 0)

    @pl.when(jnp.max(cntp) > kf)
    def _():
        need = kf - count_where(lambda blk, thr: blk > thr, tau)
        r = lax.broadcasted_iota(jnp.int32, (TK, TK), 0)
        cc = lax.broadcasted_iota(jnp.int32, (TK, TK), 1)
        before = jnp.where(cc < r, 1.0, 0.0).astype(BF16)

        def tie_body(c, seen):
            sc = score_ref[c]
            eq = sc == tau
            eqf = jnp.where(eq, 1.0, 0.0)
            rank = jnp.dot(before, eqf.astype(BF16), preferred_element_type=F32) + seen
            keep = (sc > tau) | (eq & (rank < need))
            bias_ref[c] = jnp.where(keep, 0.0, NEG_INF)
            return seen + jnp.sum(eqf, axis=0, keepdims=True)

        lax.fori_loop(0, nk, tie_body, jnp.zeros((1, TQ), F32))

    m_ref[...] = jnp.full(m_ref.shape, NEG_INF, F32)
    l_ref[...] = jnp.zeros(l_ref.shape, F32)
    acc_ref[...] = jnp.zeros(acc_ref.shape, F32)
    alpha_ref[...] = jnp.ones(alpha_ref.shape, F32)
    p1_ref[...] = jnp.zeros(p1_ref.shape, BF16)
    ones_rows = jnp.ones((BF16_SUBLANES, TK), BF16)

    def logits_head(h, kc, bias, s_out, mc_out):
        cols = slice(h * TQ, (h + 1) * TQ)
        s = jnp.dot(kc, qpad_ref[:, cols], preferred_element_type=F32) + bias
        s_out[:, cols] = s
        mc_out[h] = jnp.broadcast_to(jnp.max(s, axis=0, keepdims=True), (SUBLANES, TQ))

    def pv_update(c, p_in):
        for g in range(N_KV_HEADS):
            lanes = slice(g * rep_heads * TQ, (g + 1) * rep_heads * TQ)
            lhs = jnp.concatenate([vt_ref[0, c, g * HEAD_DIM:(g + 1) * HEAD_DIM, :], ones_rows], axis=0)
            pv = jnp.dot(lhs, p_in[:, lanes], preferred_element_type=F32)
            for hh in range(rep_heads):
                h = g * rep_heads + hh
                rows = slice(h * HEAD_DIM, (h + 1) * HEAD_DIM)
                cols = slice(hh * TQ, (hh + 1) * TQ)
                alpha = alpha_ref[h, 0:1, :]
                acc_ref[rows, :] = alpha * acc_ref[rows, :] + pv[:HEAD_DIM, cols]
                l_ref[h] = jnp.broadcast_to(alpha * l_ref[h, 0:1, :] + pv[HEAD_DIM:HEAD_DIM + 1, cols],
                                            (SUBLANES, TQ))

    def softmax_head(h, s_in, mc_in, p_out):
        m_old = m_ref[h, 0:1, :]
        m_new = jnp.maximum(m_old, mc_in[h, 0:1, :])
        m_use = jnp.where(m_new == NEG_INF, 0.0, m_new)
        alpha_ref[h] = jnp.broadcast_to(jnp.exp2(m_old - m_use), (SUBLANES, TQ))
        m_ref[h] = jnp.broadcast_to(m_new, (SUBLANES, TQ))
        cols = slice(h * TQ, (h + 1) * TQ)
        for blk in range(TK // EXP_ROWS):
            rows = slice(blk * EXP_ROWS, (blk + 1) * EXP_ROWS)
            p_out[rows, cols] = jnp.exp2(s_in[rows, cols] - m_use).astype(BF16)

    def stage(c, s_cur, mc_cur, p_cur, s_nxt, mc_nxt, p_prev):
        c_nxt = jnp.minimum(c + 1, nk - 1)
        pv_update(jnp.maximum(c - 1, 0), p_prev)
        bias_nxt = bias_ref[c_nxt]
        k_nxt = k_ref[0, c_nxt]
        for h in range(N_HEADS):
            logits_head(h, k_nxt, bias_nxt, s_nxt, mc_nxt)
            softmax_head(h, s_cur, mc_cur, p_cur)

    for h in range(N_HEADS):
        logits_head(h, k_ref[0, 0], bias_ref[0], s0_ref, mc0_ref)

    def pair_body(j, carry):
        c = 2 * j
        stage(c, s0_ref, mc0_ref, p0_ref, s1_ref, mc1_ref, p1_ref)

        @pl.when(c + 1 < nk)
        def _():
            stage(c + 1, s1_ref, mc1_ref, p1_ref, s0_ref, mc0_ref, p0_ref)

        return carry

    lax.fori_loop(0, (nk + 1) // 2, pair_body, 0)

    @pl.when((nk - 1) % 2 == 0)
    def _():
        pv_update(nk - 1, p0_ref)

    @pl.when((nk - 1) % 2 == 1)
    def _():
        pv_update(nk - 1, p1_ref)

    for h in range(N_HEADS):
        rows = slice(h * HEAD_DIM, (h + 1) * HEAD_DIM)
        acc_ref[rows, :] = acc_ref[rows, :] / l_ref[h, 0:1, :]
    yb_ref[0] = acc_ref[...].T.astype(BF16)


def _run_attn(qi, wi, ki, q, k, vt, k_sel):
    b, s, _ = q.shape
    nc = s // TK
    ki = ki.reshape(b, nc, TK, IDX_DIM)
    k = k.reshape(b, nc, TK, D_KV)
    rows = lambda width: pl.BlockSpec((1, TQ, width), lambda bi, i: (bi, i, 0))
    keys = lambda d0, d1: pl.BlockSpec((1, nc, d0, d1), lambda bi, i: (bi, 0, 0, 0))
    return pl.pallas_call(
        functools.partial(_attn_kernel, k_sel=k_sel),
        grid=(b, s // TQ),
        in_specs=[rows(IDX_HEADS * IDX_DIM), rows(LANES), keys(TK, IDX_DIM),
                  rows(D_ATTN), keys(TK, D_KV), keys(D_KV, TK)],
        out_specs=rows(D_ATTN),
        out_shape=jax.ShapeDtypeStruct((b, s, D_ATTN), BF16),
        scratch_shapes=[pltpu.VMEM((nc, TK, TQ), F32),
                        pltpu.VMEM((nc, TK, TQ), F32),
                        pltpu.VMEM((IDX_DIM, IDX_HEADS * TQ), BF16),
                        pltpu.VMEM((D_KV, N_HEADS * TQ), BF16),
                        pltpu.VMEM((TK, N_HEADS * TQ), F32),
                        pltpu.VMEM((TK, N_HEADS * TQ), F32),
                        pltpu.VMEM((TK, N_HEADS * TQ), BF16),
                        pltpu.VMEM((TK, N_HEADS * TQ), BF16),
                        pltpu.VMEM((N_HEADS, SUBLANES, TQ), F32),
                        pltpu.VMEM((N_HEADS, SUBLANES, TQ), F32),
                        pltpu.VMEM((N_HEADS, SUBLANES, TQ), F32),
                        pltpu.VMEM((N_HEADS, SUBLANES, TQ), F32),
                        pltpu.VMEM((N_HEADS, SUBLANES, TQ), F32),
                        pltpu.VMEM((D_ATTN, TQ), F32),
                        pltpu.VMEM((SUBLANES, TQ), F32),
                        pltpu.VMEM((SUBLANES, TQ), F32),
                        pltpu.VMEM((SUBLANES, TQ), F32)],
        compiler_params=pltpu.CompilerParams(
            dimension_semantics=("parallel", "parallel"), vmem_limit_bytes=VMEM_LIMIT),
        name="dsa_attention",
    )(qi, wi, ki, q, k, vt)


def _tail_kernel(x_ref, ya_ref, yb_ref, wg_ref, wbr_ref, wo_ref, wup_ref, wdn_ref,
                 g1_ref, b1_ref, g2_ref, b2_ref, out_ref, *, alpha):
    part = TM_TAIL // TAIL_PARTS
    parts = [slice(p * part, (p + 1) * part) for p in range(TAIL_PARTS)]

    def mix_pre_ln(rows):
        x = x_ref[rows, :]
        xb = x.astype(BF16)
        merged = jnp.zeros((part, D_MODEL), F32)
        for n, y_ref in enumerate((ya_ref, yb_ref)):
            gate = _dot_nt(xb, wg_ref[n * D_MODEL:(n + 1) * D_MODEL, :])
            branch = jnp.dot(y_ref[rows, :], wbr_ref[n], preferred_element_type=F32)
            merged = merged + jax.nn.sigmoid(gate) * branch
        return alpha * x + jnp.dot(merged.astype(BF16), wo_ref[...], preferred_element_type=F32)

    def mlp_pre_ln(h):
        hb = h.astype(BF16)
        ff = jnp.zeros((part, D_MODEL), F32)
        for f in range(D_FF // FF_CHUNK):
            up = jnp.dot(hb, wup_ref[:, f * FF_CHUNK:(f + 1) * FF_CHUNK], preferred_element_type=F32)
            act = jnp.square(jnp.maximum(up, 0.0)).astype(BF16)
            ff = ff + jnp.dot(act, wdn_ref[f * FF_CHUNK:(f + 1) * FF_CHUNK, :], preferred_element_type=F32)
        return alpha * h + ff

    ln1 = lambda v: _layer_norm_rows(v, g1_ref[...], b1_ref[...])
    ln2 = lambda v: _layer_norm_rows(v, g2_ref[...], b2_ref[...])
    hs = [ln1(mix_pre_ln(rows)) for rows in parts]
    for rows, h in zip(parts, hs):
        out_ref[rows, :] = ln2(mlp_pre_ln(h))


def _run_tail(x2, ya2, yb2, wg, wbr, wo, wup, wdn, g1, b1, g2, b2, alpha):
    n, d = x2.shape
    const = lambda *shape: pl.BlockSpec(shape, lambda r: (0,) * len(shape),
                                        pipeline_mode=pl.Buffered(1))
    rows = lambda width: pl.BlockSpec((TM_TAIL, width), lambda r: (r, 0))
    return pl.pallas_call(
        functools.partial(_tail_kernel, alpha=alpha),
        grid=(n // TM_TAIL,),
        in_specs=[rows(d), rows(D_CONV), rows(D_ATTN),
                  const(N_BRANCH * d, d), const(N_BRANCH, D_CONV, d), const(d, d),
                  const(d, D_FF), const(D_FF, d),
                  const(1, d), const(1, d), const(1, d), const(1, d)],
        out_specs=rows(d),
        out_shape=jax.ShapeDtypeStruct((n, d), F32),
        compiler_params=pltpu.CompilerParams(
            dimension_semantics=("parallel",), vmem_limit_bytes=VMEM_LIMIT),
        name="merge_mlp",
    )(x2, ya2, yb2, wg, wbr, wo, wup, wdn, g1, b1, g2, b2)


def kernel(x, w_in, conv_w, idx_k_norm_g, idx_k_norm_b, w_branch, w_o, ln1_g, ln1_b, w_up, w_down, ln2_g, ln2_b):
    b, s, d = x.shape
    depth = w_in.shape[0]
    assert d == D_MODEL and s % TM_PROJ == 0 and (b * s) % TM_TAIL == 0 and TQ == TK
    k_sel = min(TOPK_MAX, s // 4)
    alpha = (2 * depth) ** 0.25
    for l in range(depth):
        wt = jnp.swapaxes(w_in[l], 0, 1)
        wa = wt[:OFF_KI].astype(BF16)
        wkw = jnp.concatenate(
            [wt[OFF_KI:OFF_WI], jnp.zeros((LANES - IDX_DIM, d), F32),
             wt[OFF_WI:OFF_GATES], jnp.zeros((LANES - IDX_HEADS, d), F32)], axis=0).astype(BF16)
        wg = wt[OFF_GATES:].astype(BF16)
        pad = lambda a: jnp.pad(a.astype(F32), (0, LANES - IDX_DIM)).reshape(1, LANES)
        later_weights = (w_branch[l].reshape(N_BRANCH * D_CONV, d), w_o[l], w_up[l], w_down[l])
        ya, q, k, vt, qi, ki, wi, wbr, wo, wup, wdn = _run_proj(
            x, wa, wkw, conv_w[l], pad(idx_k_norm_g[l]), pad(idx_k_norm_b[l]), later_weights)
        yb = _run_attn(qi, wi, ki, q, k, vt, k_sel)
        row = lambda a: a.astype(F32).reshape(1, d)
        out = _run_tail(x.reshape(b * s, d), ya.reshape(b * s, D_CONV), yb.reshape(b * s, D_ATTN),
                        wg, wbr.reshape(N_BRANCH, D_CONV, d), wo, wup, wdn,
                        row(ln1_g[l]), row(ln1_b[l]), row(ln2_g[l]), row(ln2_b[l]), alpha)
        x = out.reshape(b, s, d)
    return x
```

```python
import functools

import jax
import jax.numpy as jnp
from jax import lax
from jax.experimental import pallas as pl
from jax.experimental.pallas import tpu as pltpu

D_MODEL = 1024
D_CONV = 512
CONV_WIDTH = 3
N_HEADS = 8
N_KV_HEADS = 2
HEAD_DIM = 64
D_ATTN = N_HEADS * HEAD_DIM
D_KV = N_KV_HEADS * HEAD_DIM
IDX_HEADS = 8
IDX_DIM = 64
TOPK_MAX = 256
N_BRANCH = 2
D_FF = 4 * D_MODEL
LN_EPS = 1e-5
IDX_SCALE = (IDX_DIM ** -0.5) * (IDX_HEADS ** -0.5)
ATTN_SCALE = HEAD_DIM ** -0.5
LOG2_E = 1.4426950408889634
Q_SCALE = ATTN_SCALE * LOG2_E

OFF_Q = 3 * D_CONV
OFF_K = OFF_Q + D_ATTN
OFF_V = OFF_K + D_KV
OFF_QI = OFF_V + D_KV
OFF_KI = OFF_QI + IDX_HEADS * IDX_DIM
OFF_WI = OFF_KI + IDX_DIM
OFF_GATES = OFF_WI + IDX_HEADS

LANES = 128
SUBLANES = 8
BF16_SUBLANES = 16
VMEM_LIMIT = 56 * 1024 * 1024

TM_PROJ = 1024
TQ = 256
TK = 256
TM_TAIL = 512
TAIL_PARTS = 2
FF_CHUNK = 1024
COUNT_CHAINS = 4
EXP_ROWS = 32

F32 = jnp.float32
BF16 = jnp.bfloat16
NEG_INF = float("-inf")
F32_MAX = float(jnp.finfo(jnp.float32).max)
INT_MIN = -2 ** 31
KEY_BITS = 32
CAST_ROWS = (N_BRANCH * D_CONV, D_MODEL, D_MODEL, D_FF)


def _dot_nt(a, b_t):
    return lax.dot_general(a, b_t, (((1,), (1,)), ((), ())), preferred_element_type=F32)


def _layer_norm_rows(x, g, b):
    mu = jnp.mean(x, axis=-1, keepdims=True)
    xc = x - mu
    var = jnp.mean(xc * xc, axis=-1, keepdims=True)
    return xc * lax.rsqrt(var + LN_EPS) * g + b


def _proj_kernel(x_ref, wt_ref, convw_ref, kg_ref, kb_ref, *rest):
    n_cast = len(CAST_ROWS)
    cast_in, rest = rest[:n_cast], rest[n_cast:]
    ya_ref, q_ref, k_ref, vt_ref, qi_ref, ki_ref, wi_ref, wg_ref = rest[:8]
    cast_out, carry_ref = rest[8:8 + n_cast], rest[8 + n_cast]
    for src, dst in zip(cast_in, cast_out):
        dst[...] = src[...].astype(BF16)
    step = pl.program_id(0) * pl.num_programs(1) + pl.program_id(1)
    gate_rows = wg_ref.shape[0]
    wg_ref[...] = wt_ref[pl.ds(pl.multiple_of(OFF_GATES + step * gate_rows, SUBLANES), gate_rows), :].astype(BF16)

    @pl.when(pl.program_id(1) == 0)
    def _():
        carry_ref[...] = jnp.zeros_like(carry_ref)

    xb = x_ref[0].astype(BF16)

    def proj(lo, hi):
        return _dot_nt(xb, wt_ref[lo:hi, :].astype(BF16))

    zk = _dot_nt(xb, wt_ref[OFF_KI:OFF_KI + LANES, :].astype(BF16))
    wi_ref[0] = zk
    valid = lax.broadcasted_iota(jnp.int32, zk.shape, 1) < IDX_DIM
    mu = jnp.sum(jnp.where(valid, zk, 0.0), axis=-1, keepdims=True) * (1.0 / IDX_DIM)
    xc = jnp.where(valid, zk - mu, 0.0)
    var = jnp.sum(xc * xc, axis=-1, keepdims=True) * (1.0 / IDX_DIM)
    kin = xc * lax.rsqrt(var + LN_EPS) * kg_ref[...] + kb_ref[...]
    ki_ref[0] = kin[:, :IDX_DIM].astype(BF16)

    vt = proj(OFF_V, OFF_QI).T
    for c in range(TM_PROJ // TK):
        vt_ref[0, c] = vt[:, c * TK:(c + 1) * TK].astype(BF16)

    cu = proj(D_CONV, 2 * D_CONV) * proj(2 * D_CONV, 3 * D_CONV)
    prev = carry_ref[...]
    row = lax.broadcasted_iota(jnp.int32, cu.shape, 0)
    cu1 = jnp.where(row == 0, prev[SUBLANES - 1:SUBLANES, :], pltpu.roll(cu, 1, axis=0))
    cu2 = jnp.where(row == 0, prev[SUBLANES - 2:SUBLANES - 1, :],
                    jnp.where(row == 1, prev[SUBLANES - 1:SUBLANES, :], pltpu.roll(cu, 2, axis=0)))
    carry_ref[...] = cu[TM_PROJ - SUBLANES:, :]
    cw = convw_ref[...]
    conv = cw[0:1, :] * cu2 + cw[1:2, :] * cu1 + cw[2:3, :] * cu
    ya_ref[0] = (proj(0, D_CONV) * conv).astype(BF16)

    k_ref[0] = proj(OFF_K, OFF_V).astype(BF16)
    q_ref[0] = (proj(OFF_Q, OFF_K) * Q_SCALE).astype(BF16)
    qi_ref[0] = proj(OFF_QI, OFF_KI).astype(BF16)


def _run_proj(x, wt, conv_w, kg, kb, cast_weights):
    b, s, d = x.shape
    nc = s // TK
    cpt = TM_PROJ // TK
    steps = s // TM_PROJ
    n_steps = b * steps
    assert tuple(w.shape[0] for w in cast_weights) == CAST_ROWS
    gate_rows = wt.shape[0] - OFF_GATES
    cast_specs = [pl.BlockSpec((w.shape[0] // n_steps, w.shape[1]), lambda bi, j: (bi * steps + j, 0))
                  for w in cast_weights]
    const = lambda *shape: pl.BlockSpec(shape, lambda bi, j: (0,) * len(shape),
                                        pipeline_mode=pl.Buffered(1))
    rows = lambda width: pl.BlockSpec((1, TM_PROJ, width), lambda bi, j: (bi, j, 0))
    return pl.pallas_call(
        _proj_kernel,
        grid=(b, s // TM_PROJ),
        in_specs=[rows(d), const(*wt.shape), const(CONV_WIDTH, D_CONV),
                  const(1, LANES), const(1, LANES)] + cast_specs,
        out_specs=[rows(D_CONV), rows(D_ATTN), rows(D_KV),
                   pl.BlockSpec((1, cpt, D_KV, TK), lambda bi, j: (bi, j, 0, 0)),
                   rows(IDX_HEADS * IDX_DIM), rows(IDX_DIM), rows(LANES),
                   pl.BlockSpec((gate_rows // n_steps, d), lambda bi, j: (bi * steps + j, 0))] + cast_specs,
        out_shape=[jax.ShapeDtypeStruct((b, s, D_CONV), BF16),
                   jax.ShapeDtypeStruct((b, s, D_ATTN), BF16),
                   jax.ShapeDtypeStruct((b, s, D_KV), BF16),
                   jax.ShapeDtypeStruct((b, nc, D_KV, TK), BF16),
                   jax.ShapeDtypeStruct((b, s, IDX_HEADS * IDX_DIM), BF16),
                   jax.ShapeDtypeStruct((b, s, IDX_DIM), BF16),
                   jax.ShapeDtypeStruct((b, s, LANES), F32),
                   jax.ShapeDtypeStruct((gate_rows, d), BF16)]
        + [jax.ShapeDtypeStruct(w.shape, BF16) for w in cast_weights],
        scratch_shapes=[pltpu.VMEM((SUBLANES, D_CONV), F32)],
        compiler_params=pltpu.CompilerParams(
            dimension_semantics=("parallel", "arbitrary"), vmem_limit_bytes=VMEM_LIMIT),
        name="proj_mixer_a",
    )(x, wt, conv_w, kg, kb, *cast_weights)


def _sortable_to_f32(x):
    bits = x ^ ((x >> 31) & jnp.int32(0x7FFFFFFF))
    return pltpu.bitcast(bits, F32)


def _attn_kernel(qi_ref, wi_ref, ki_ref, q_ref, k_ref, vt_ref, yb_ref,
                 score_ref, bias_ref, qit_ref, qpad_ref, s0_ref, s1_ref, p0_ref, p1_ref,
                 mc0_ref, mc1_ref, alpha_ref, m_ref, l_ref, acc_ref, thr_ref, tau_ref, cntp_ref, *, k_sel):
    i = pl.program_id(1)
    nk = i + 1
    kf = float(k_sel)
    rep_heads = N_HEADS // N_KV_HEADS
    n_groups = TK // SUBLANES
    n_lane_groups = TQ // LANES

    qit = qi_ref[0].astype(F32).T.astype(BF16)
    for h in range(IDX_HEADS):
        qit_ref[:, h * TQ:(h + 1) * TQ] = qit[h * IDX_DIM:(h + 1) * IDX_DIM, :]
    wt = wi_ref[0].T[IDX_DIM:IDX_DIM + IDX_HEADS, :]
    qt = q_ref[0].astype(F32).T.astype(BF16)
    zero_half = jnp.zeros((HEAD_DIM, TQ), BF16)
    for h in range(N_HEADS):
        g = h // rep_heads
        for gg in range(N_KV_HEADS):
            qpad_ref[gg * HEAD_DIM:(gg + 1) * HEAD_DIM, h * TQ:(h + 1) * TQ] = (
                qt[h * HEAD_DIM:(h + 1) * HEAD_DIM, :] if gg == g else zero_half)

    row = lax.broadcasted_iota(jnp.int32, (TK, TQ), 0)
    col = lax.broadcasted_iota(jnp.int32, (TK, TQ), 1)
    row_minus_col = row - col

    def score_chunk(c):
        kic = ki_ref[0, c]
        lg_all = jnp.dot(kic, qit_ref[...], preferred_element_type=F32)
        acc = jnp.zeros((TK, TQ), F32)
        for h in range(IDX_HEADS):
            lg = lg_all[:, h * TQ:(h + 1) * TQ]
            acc = acc + wt[h:h + 1, :] * jnp.maximum(lg, 0.0)
        noncausal = row_minus_col > (i - c) * TQ
        score_ref[c] = jnp.where(noncausal, NEG_INF, acc * IDX_SCALE)

    def score_pair(j, carry):
        score_chunk(2 * j)
        score_chunk(2 * j + 1)
        return carry

    lax.fori_loop(0, nk // 2, score_pair, 0)

    @pl.when(nk % 2 == 1)
    def _():
        score_chunk(nk - 1)

    def count_where(pred, thr):
        thr_ref[...] = jnp.broadcast_to(thr, (SUBLANES, TQ))
        thr_g = [thr_ref[:, g * LANES:(g + 1) * LANES] for g in range(n_lane_groups)]

        def chunk_counts(c, accs, diagonal):
            accs = [list(a) for a in accs]
            for r in range(n_groups):
                for g in range(n_lane_groups):
                    if diagonal and r * SUBLANES >= (g + 1) * LANES:
                        continue
                    blk = score_ref[c, r * SUBLANES:(r + 1) * SUBLANES, g * LANES:(g + 1) * LANES]
                    a = accs[g][r % COUNT_CHAINS]
                    accs[g][r % COUNT_CHAINS] = jnp.where(pred(blk, thr_g[g]), a + 1.0, a)
            return tuple(tuple(a) for a in accs)

        zero = jnp.zeros((SUBLANES, LANES), F32)
        accs = ((zero,) * COUNT_CHAINS,) * n_lane_groups
        accs = lax.fori_loop(0, nk - 1, lambda c, a: chunk_counts(c, a, False), accs)
        accs = chunk_counts(nk - 1, accs, True)
        per_group = [sum(a[1:], a[0]) for a in accs]
        return jnp.sum(jnp.concatenate(per_group, axis=1), axis=0, keepdims=True)

    def count_ge(cand_key):
        return count_where(lambda blk, thr: blk >= thr, _sortable_to_f32(cand_key))

    def bit_body(b, carry):
        prefix, cntp = carry
        cand_key = prefix | jnp.left_shift(jnp.int32(1), KEY_BITS - 2 - b)
        cnt = count_ge(cand_key)
        ok = cnt >= kf
        return jnp.where(ok, cand_key, prefix), jnp.where(ok, cnt, cntp)

    tau_ref[...] = jnp.full(tau_ref.shape, -F32_MAX, F32)
    cntp_ref[...] = jnp.zeros(cntp_ref.shape, F32)

    @pl.when(nk * TK > k_sel)
    def _():
        cnt0 = count_ge(jnp.zeros((1, TQ), jnp.int32))
        ok0 = cnt0 >= kf
        prefix0 = jnp.where(ok0, jnp.int32(0), jnp.int32(INT_MIN))
        prefix, cntp = lax.fori_loop(0, KEY_BITS - 1, bit_body, (prefix0, jnp.where(ok0, cnt0, 0.0)))
        tau_found = jnp.where(prefix == INT_MIN, -F32_MAX, _sortable_to_f32(prefix))
        tau_ref[...] = jnp.broadcast_to(tau_found, tau_ref.shape)
        cntp_ref[...] = jnp.broadcast_to(cntp, cntp_ref.shape)

    tau = tau_ref[0:1, :]
    cntp = cntp_ref[0:1, :]

    def bias_body(c, carry):
        bias_ref[c] = jnp.where(score_ref[c] >= tau, 0.0, NEG_INF)
        return carry

    lax.fori_loop(0, nk, bias_body, 0)

    @pl.when(jnp.max(cntp) > kf)
    def _():
        need = kf - count_where(lambda blk, thr: blk > thr, tau)
        r = lax.broadcasted_iota(jnp.int32, (TK, TK), 0)
        cc = lax.broadcasted_iota(jnp.int32, (TK, TK), 1)
        before = jnp.where(cc < r, 1.0, 0.0).astype(BF16)

        def tie_body(c, seen):
            sc = score_ref[c]
            eq = sc == tau
            eqf = jnp.where(eq, 1.0, 0.0)
            rank = jnp.dot(before, eqf.astype(BF16), preferred_element_type=F32) + seen
            keep = (sc > tau) | (eq & (rank < need))
            bias_ref[c] = jnp.where(keep, 0.0, NEG_INF)
            return seen + jnp.sum(eqf, axis=0, keepdims=True)

        lax.fori_loop(0, nk, tie_body, jnp.zeros((1, TQ), F32))

    m_ref[...] = jnp.full(m_ref.shape, NEG_INF, F32)
    l_ref[...] = jnp.zeros(l_ref.shape, F32)
    acc_ref[...] = jnp.zeros(acc_ref.shape, F32)
    alpha_ref[...] = jnp.ones(alpha_ref.shape, F32)
    p1_ref[...] = jnp.zeros(p1_ref.shape, BF16)
    ones_rows = jnp.ones((BF16_SUBLANES, TK), BF16)

    def logits_head(h, kc, bias, s_out, mc_out):
        cols = slice(h * TQ, (h + 1) * TQ)
        s = jnp.dot(kc, qpad_ref[:, cols], preferred_element_type=F32) + bias
        s_out[:, cols] = s
        mc_out[h] = jnp.broadcast_to(jnp.max(s, axis=0, keepdims=True), (SUBLANES, TQ))

    def pv_update(c, p_in):
        for g in range(N_KV_HEADS):
            lanes = slice(g * rep_heads * TQ, (g + 1) * rep_heads * TQ)
            lhs = jnp.concatenate([vt_ref[0, c, g * HEAD_DIM:(g + 1) * HEAD_DIM, :], ones_rows], axis=0)
            pv = jnp.dot(lhs, p_in[:, lanes], preferred_element_type=F32)
            for hh in range(rep_heads):
                h = g * rep_heads + hh
                rows = slice(h * HEAD_DIM, (h + 1) * HEAD_DIM)
                cols = slice(hh * TQ, (hh + 1) * TQ)
                alpha = alpha_ref[h, 0:1, :]
                acc_ref[rows, :] = alpha * acc_ref[rows, :] + pv[:HEAD_DIM, cols]
                l_ref[h] = jnp.broadcast_to(alpha * l_ref[h, 0:1, :] + pv[HEAD_DIM:HEAD_DIM + 1, cols],
                                            (SUBLANES, TQ))

    def softmax_head(h, s_in, mc_in, p_out):
        m_old = m_ref[h, 0:1, :]
        m_new = jnp.maximum(m_old, mc_in[h, 0:1, :])
        m_use = jnp.where(m_new == NEG_INF, 0.0, m_new)
        alpha_ref[h] = jnp.broadcast_to(jnp.exp2(m_old - m_use), (SUBLANES, TQ))
        m_ref[h] = jnp.broadcast_to(m_new, (SUBLANES, TQ))
        cols = slice(h * TQ, (h + 1) * TQ)
        for blk in range(TK // EXP_ROWS):
            rows = slice(blk * EXP_ROWS, (blk + 1) * EXP_ROWS)
            p_out[rows, cols] = jnp.exp2(s_in[rows, cols] - m_use).astype(BF16)

    def stage(c, s_cur, mc_cur, p_cur, s_nxt, mc_nxt, p_prev):
        c_nxt = jnp.minimum(c + 1, nk - 1)
        pv_update(jnp.maximum(c - 1, 0), p_prev)
        bias_nxt = bias_ref[c_nxt]
        k_nxt = k_ref[0, c_nxt]
        for h in range(N_HEADS):
            logits_head(h, k_nxt, bias_nxt, s_nxt, mc_nxt)
            softmax_head(h, s_cur, mc_cur, p_cur)

    for h in range(N_HEADS):
        logits_head(h, k_ref[0, 0], bias_ref[0], s0_ref, mc0_ref)

    def pair_body(j, carry):
        c = 2 * j
        stage(c, s0_ref, mc0_ref, p0_ref, s1_ref, mc1_ref, p1_ref)

        @pl.when(c + 1 < nk)
        def _():
            stage(c + 1, s1_ref, mc1_ref, p1_ref, s0_ref, mc0_ref, p0_ref)

        return carry

    lax.fori_loop(0, (nk + 1) // 2, pair_body, 0)

    @pl.when((nk - 1) % 2 == 0)
    def _():
        pv_update(nk - 1, p0_ref)

    @pl.when((nk - 1) % 2 == 1)
    def _():
        pv_update(nk - 1, p1_ref)

    for h in range(N_HEADS):
        rows = slice(h * HEAD_DIM, (h + 1) * HEAD_DIM)
        acc_ref[rows, :] = acc_ref[rows, :] / l_ref[h, 0:1, :]
    yb_ref[0] = acc_ref[...].T.astype(BF16)


def _run_attn(qi, wi, ki, q, k, vt, k_sel):
    b, s, _ = q.shape
    nc = s // TK
    ki = ki.reshape(b, nc, TK, IDX_DIM)
    k = k.reshape(b, nc, TK, D_KV)
    rows = lambda width: pl.BlockSpec((1, TQ, width), lambda bi, i: (bi, i, 0))
    keys = lambda d0, d1: pl.BlockSpec((1, nc, d0, d1), lambda bi, i: (bi, 0, 0, 0))
    return pl.pallas_call(
        functools.partial(_attn_kernel, k_sel=k_sel),
        grid=(b, s // TQ),
        in_specs=[rows(IDX_HEADS * IDX_DIM), rows(LANES), keys(TK, IDX_DIM),
                  rows(D_ATTN), keys(TK, D_KV), keys(D_KV, TK)],
        out_specs=rows(D_ATTN),
        out_shape=jax.ShapeDtypeStruct((b, s, D_ATTN), BF16),
        scratch_shapes=[pltpu.VMEM((nc, TK, TQ), F32),
                        pltpu.VMEM((nc, TK, TQ), F32),
                        pltpu.VMEM((IDX_DIM, IDX_HEADS * TQ), BF16),
                        pltpu.VMEM((D_KV, N_HEADS * TQ), BF16),
                        pltpu.VMEM((TK, N_HEADS * TQ), F32),
                        pltpu.VMEM((TK, N_HEADS * TQ), F32),
                        pltpu.VMEM((TK, N_HEADS * TQ), BF16),
                        pltpu.VMEM((TK, N_HEADS * TQ), BF16),
                        pltpu.VMEM((N_HEADS, SUBLANES, TQ), F32),
                        pltpu.VMEM((N_HEADS, SUBLANES, TQ), F32),
                        pltpu.VMEM((N_HEADS, SUBLANES, TQ), F32),
                        pltpu.VMEM((N_HEADS, SUBLANES, TQ), F32),
                        pltpu.VMEM((N_HEADS, SUBLANES, TQ), F32),
                        pltpu.VMEM((D_ATTN, TQ), F32),
                        pltpu.VMEM((SUBLANES, TQ), F32),
                        pltpu.VMEM((SUBLANES, TQ), F32),
                        pltpu.VMEM((SUBLANES, TQ), F32)],
        compiler_params=pltpu.CompilerParams(
            dimension_semantics=("parallel", "parallel"), vmem_limit_bytes=VMEM_LIMIT),
        name="dsa_attention",
    )(qi, wi, ki, q, k, vt)


def _tail_kernel(x_ref, ya_ref, yb_ref, wg_ref, wbr_ref, wo_ref, wup_ref, wdn_ref,
                 g1_ref, b1_ref, g2_ref, b2_ref, out_ref, *, alpha):
    part = TM_TAIL // TAIL_PARTS
    parts = [slice(p * part, (p + 1) * part) for p in range(TAIL_PARTS)]

    def mix_pre_ln(rows):
        x = x_ref[rows, :]
        xb = x.astype(BF16)
        merged = jnp.zeros((part, D_MODEL), F32)
        for n, y_ref in enumerate((ya_ref, yb_ref)):
            gate = _dot_nt(xb, wg_ref[n * D_MODEL:(n + 1) * D_MODEL, :])
            branch = jnp.dot(y_ref[rows, :], wbr_ref[n], preferred_element_type=F32)
            merged = merged + jax.nn.sigmoid(gate) * branch
        return alpha * x + jnp.dot(merged.astype(BF16), wo_ref[...], preferred_element_type=F32)

    def mlp_pre_ln(h):
        hb = h.astype(BF16)
        ff = jnp.zeros((part, D_MODEL), F32)
        for f in range(D_FF // FF_CHUNK):
            up = jnp.dot(hb, wup_ref[:, f * FF_CHUNK:(f + 1) * FF_CHUNK], preferred_element_type=F32)
            act = jnp.square(jnp.maximum(up, 0.0)).astype(BF16)
            ff = ff + jnp.dot(act, wdn_ref[f * FF_CHUNK:(f + 1) * FF_CHUNK, :], preferred_element_type=F32)
        return alpha * h + ff

    ln1 = lambda v: _layer_norm_rows(v, g1_ref[...], b1_ref[...])
    ln2 = lambda v: _layer_norm_rows(v, g2_ref[...], b2_ref[...])
    hs = [ln1(mix_pre_ln(rows)) for rows in parts]
    for rows, h in zip(parts, hs):
        out_ref[rows, :] = ln2(mlp_pre_ln(h))


def _run_tail(x2, ya2, yb2, wg, wbr, wo, wup, wdn, g1, b1, g2, b2, alpha):
    n, d = x2.shape
    const = lambda *shape: pl.BlockSpec(shape, lambda r: (0,) * len(shape),
                                        pipeline_mode=pl.Buffered(1))
    rows = lambda width: pl.BlockSpec((TM_TAIL, width), lambda r: (r, 0))
    return pl.pallas_call(
        functools.partial(_tail_kernel, alpha=alpha),
        grid=(n // TM_TAIL,),
        in_specs=[rows(d), rows(D_CONV), rows(D_ATTN),
                  const(N_BRANCH * d, d), const(N_BRANCH, D_CONV, d), const(d, d),
                  const(d, D_FF), const(D_FF, d),
                  const(1, d), const(1, d), const(1, d), const(1, d)],
        out_specs=rows(d),
        out_shape=jax.ShapeDtypeStruct((n, d), F32),
        compiler_params=pltpu.CompilerParams(
            dimension_semantics=("parallel",), vmem_limit_bytes=VMEM_LIMIT),
        name="merge_mlp",
    )(x2, ya2, yb2, wg, wbr, wo, wup, wdn, g1, b1, g2, b2)


def kernel(x, w_in, conv_w, idx_k_norm_g, idx_k_norm_b, w_branch, w_o, ln1_g, ln1_b, w_up, w_down, ln2_g, ln2_b):
    b, s, d = x.shape
    depth = w_in.shape[0]
    assert d == D_MODEL and s % TM_PROJ == 0 and (b * s) % TM_TAIL == 0 and TQ == TK
    k_sel = min(TOPK_MAX, s // 4)
    alpha = (2 * depth) ** 0.25
    for l in range(depth):
        wt = jnp.swapaxes(w_in[l], 0, 1)
        pad = lambda a: jnp.pad(a.astype(F32), (0, LANES - IDX_DIM)).reshape(1, LANES)
        later_weights = (w_branch[l].reshape(N_BRANCH * D_CONV, d), w_o[l], w_up[l], w_down[l])
        ya, q, k, vt, qi, ki, wi, wg, wbr, wo, wup, wdn = _run_proj(
            x, wt, conv_w[l], pad(idx_k_norm_g[l]), pad(idx_k_norm_b[l]), later_weights)
        yb = _run_attn(qi, wi, ki, q, k, vt, k_sel)
        row = lambda a: a.astype(F32).reshape(1, d)
        out = _run_tail(x.reshape(b * s, d), ya.reshape(b * s, D_CONV), yb.reshape(b * s, D_ATTN),
                        wg, wbr.reshape(N_BRANCH, D_CONV, d), wo, wup, wdn,
                        row(ln1_g[l]), row(ln1_b[l]), row(ln2_g[l]), row(ln2_b[l]), alpha)
        x = out.reshape(b, s, d)
    return x
```

```python
import functools

import jax
import jax.numpy as jnp
from jax import lax
from jax.experimental import pallas as pl
from jax.experimental.pallas import tpu as pltpu

D_MODEL = 1024
D_CONV = 512
CONV_WIDTH = 3
N_HEADS = 8
N_KV_HEADS = 2
HEAD_DIM = 64
D_ATTN = N_HEADS * HEAD_DIM
D_KV = N_KV_HEADS * HEAD_DIM
IDX_HEADS = 8
IDX_DIM = 64
TOPK_MAX = 256
N_BRANCH = 2
D_FF = 4 * D_MODEL
LN_EPS = 1e-5
IDX_SCALE = (IDX_DIM ** -0.5) * (IDX_HEADS ** -0.5)
ATTN_SCALE = HEAD_DIM ** -0.5
LOG2_E = 1.4426950408889634
Q_SCALE = ATTN_SCALE * LOG2_E

OFF_Q = 3 * D_CONV
OFF_K = OFF_Q + D_ATTN
OFF_V = OFF_K + D_KV
OFF_QI = OFF_V + D_KV
OFF_KI = OFF_QI + IDX_HEADS * IDX_DIM
OFF_WI = OFF_KI + IDX_DIM
OFF_GATES = OFF_WI + IDX_HEADS

LANES = 128
SUBLANES = 8
BF16_SUBLANES = 16
VMEM_LIMIT = 56 * 1024 * 1024

TM_PROJ = 1024
TQ = 256
TK = 256
TM_TAIL = 512
TAIL_PARTS = 2
FF_CHUNK = 1024
COUNT_CHAINS = 4
EXP_ROWS = 32

F32 = jnp.float32
BF16 = jnp.bfloat16
NEG_INF = float("-inf")
F32_MAX = float(jnp.finfo(jnp.float32).max)
INT_MIN = -2 ** 31
KEY_BITS = 32
CAST_ROWS = (N_BRANCH * D_CONV, D_MODEL, D_MODEL, D_FF)


def _dot_nt(a, b_t):
    return lax.dot_general(a, b_t, (((1,), (1,)), ((), ())), preferred_element_type=F32)


def _layer_norm_rows(x, g, b):
    mu = jnp.mean(x, axis=-1, keepdims=True)
    xc = x - mu
    var = jnp.mean(xc * xc, axis=-1, keepdims=True)
    return xc * lax.rsqrt(var + LN_EPS) * g + b


def _proj_kernel(x_ref, wt_ref, convw_ref, kg_ref, kb_ref, *rest):
    n_cast = len(CAST_ROWS)
    cast_in, rest = rest[:n_cast], rest[n_cast:]
    ya_ref, q_ref, k_ref, vt_ref, qi_ref, ki_ref, wi_ref, wg_ref = rest[:8]
    cast_out, carry_ref = rest[8:8 + n_cast], rest[8 + n_cast]
    for src, dst in zip(cast_in, cast_out):
        dst[...] = src[...].astype(BF16)
    step = pl.program_id(0) * pl.num_programs(1) + pl.program_id(1)
    gate_rows = wg_ref.shape[0]
    wg_ref[...] = wt_ref[pl.ds(pl.multiple_of(OFF_GATES + step * gate_rows, SUBLANES), gate_rows), :].astype(BF16)

    @pl.when(pl.program_id(1) == 0)
    def _():
        carry_ref[...] = jnp.zeros_like(carry_ref)

    xb = x_ref[0].astype(BF16)

    def proj(lo, hi):
        return _dot_nt(xb, wt_ref[lo:hi, :].astype(BF16))

    zk = _dot_nt(xb, wt_ref[OFF_KI:OFF_KI + LANES, :].astype(BF16))
    wi_ref[0] = zk
    valid = lax.broadcasted_iota(jnp.int32, zk.shape, 1) < IDX_DIM
    mu = jnp.sum(jnp.where(valid, zk, 0.0), axis=-1, keepdims=True) * (1.0 / IDX_DIM)
    xc = jnp.where(valid, zk - mu, 0.0)
    var = jnp.sum(xc * xc, axis=-1, keepdims=True) * (1.0 / IDX_DIM)
    kin = xc * lax.rsqrt(var + LN_EPS) * kg_ref[...] + kb_ref[...]
    ki_ref[0] = kin[:, :IDX_DIM].astype(BF16)

    vt = proj(OFF_V, OFF_QI).T
    for c in range(TM_PROJ // TK):
        vt_ref[0, c] = vt[:, c * TK:(c + 1) * TK].astype(BF16)

    cu = proj(D_CONV, 2 * D_CONV) * proj(2 * D_CONV, 3 * D_CONV)
    prev = carry_ref[...]
    row = lax.broadcasted_iota(jnp.int32, cu.shape, 0)
    cu1 = jnp.where(row == 0, prev[SUBLANES - 1:SUBLANES, :], pltpu.roll(cu, 1, axis=0))
    cu2 = jnp.where(row == 0, prev[SUBLANES - 2:SUBLANES - 1, :],
                    jnp.where(row == 1, prev[SUBLANES - 1:SUBLANES, :], pltpu.roll(cu, 2, axis=0)))
    carry_ref[...] = cu[TM_PROJ - SUBLANES:, :]
    cw = convw_ref[...]
    conv = cw[0:1, :] * cu2 + cw[1:2, :] * cu1 + cw[2:3, :] * cu
    ya_ref[0] = (proj(0, D_CONV) * conv).astype(BF16)

    k_ref[0] = proj(OFF_K, OFF_V).astype(BF16)
    q_ref[0] = (proj(OFF_Q, OFF_K) * Q_SCALE).astype(BF16)
    qi_ref[0] = proj(OFF_QI, OFF_KI).astype(BF16)


def _run_proj(x, wt, conv_w, kg, kb, cast_weights):
    b, s, d = x.shape
    nc = s // TK
    cpt = TM_PROJ // TK
    steps = s // TM_PROJ
    n_steps = b * steps
    assert tuple(w.shape[0] for w in cast_weights) == CAST_ROWS
    gate_rows = wt.shape[0] - OFF_GATES
    cast_specs = [pl.BlockSpec((w.shape[0] // n_steps, w.shape[1]), lambda bi, j: (bi * steps + j, 0))
                  for w in cast_weights]
    const = lambda *shape: pl.BlockSpec(shape, lambda bi, j: (0,) * len(shape),
                                        pipeline_mode=pl.Buffered(1))
    rows = lambda width: pl.BlockSpec((1, TM_PROJ, width), lambda bi, j: (bi, j, 0))
    return pl.pallas_call(
        _proj_kernel,
        grid=(b, s // TM_PROJ),
        in_specs=[rows(d), const(*wt.shape), const(CONV_WIDTH, D_CONV),
                  const(1, LANES), const(1, LANES)] + cast_specs,
        out_specs=[rows(D_CONV), rows(D_ATTN), rows(D_KV),
                   pl.BlockSpec((1, cpt, D_KV, TK), lambda bi, j: (bi, j, 0, 0)),
                   rows(IDX_HEADS * IDX_DIM), rows(IDX_DIM), rows(LANES),
                   pl.BlockSpec((gate_rows // n_steps, d), lambda bi, j: (bi * steps + j, 0))] + cast_specs,
        out_shape=[jax.ShapeDtypeStruct((b, s, D_CONV), BF16),
                   jax.ShapeDtypeStruct((b, s, D_ATTN), BF16),
                   jax.ShapeDtypeStruct((b, s, D_KV), BF16),
                   jax.ShapeDtypeStruct((b, nc, D_KV, TK), BF16),
                   jax.ShapeDtypeStruct((b, s, IDX_HEADS * IDX_DIM), BF16),
                   jax.ShapeDtypeStruct((b, s, IDX_DIM), BF16),
                   jax.ShapeDtypeStruct((b, s, LANES), F32),
                   jax.ShapeDtypeStruct((gate_rows, d), BF16)]
        + [jax.ShapeDtypeStruct(w.shape, BF16) for w in cast_weights],
        scratch_shapes=[pltpu.VMEM((SUBLANES, D_CONV), F32)],
        compiler_params=pltpu.CompilerParams(
            dimension_semantics=("parallel", "arbitrary"), vmem_limit_bytes=VMEM_LIMIT),
        name="proj_mixer_a",
    )(x, wt, conv_w, kg, kb, *cast_weights)


def _sortable_to_f32(x):
    bits = x ^ ((x >> 31) & jnp.int32(0x7FFFFFFF))
    return pltpu.bitcast(bits, F32)


def _attn_kernel(qi_ref, wi_ref, ki_ref, q_ref, k_ref, vt_ref, yb_ref,
                 score_ref, bias_ref, qit_ref, qpad_ref, s0_ref, s1_ref, p0_ref, p1_ref,
                 mc0_ref, mc1_ref, alpha_ref, m_ref, l_ref, acc_ref, thr_ref, tau_ref, cntp_ref, *, k_sel):
    i = pl.program_id(1)
    nk = i + 1
    kf = float(k_sel)
    rep_heads = N_HEADS // N_KV_HEADS
    n_groups = TK // SUBLANES
    n_lane_groups = TQ // LANES

    qit = qi_ref[0].astype(F32).T.astype(BF16)
    for h in range(IDX_HEADS):
        qit_ref[:, h * TQ:(h + 1) * TQ] = qit[h * IDX_DIM:(h + 1) * IDX_DIM, :]
    wt = wi_ref[0].T[IDX_DIM:IDX_DIM + IDX_HEADS, :]
    qt = q_ref[0].astype(F32).T.astype(BF16)
    zero_half = jnp.zeros((HEAD_DIM, TQ), BF16)
    for h in range(N_HEADS):
        g = h // rep_heads
        for gg in range(N_KV_HEADS):
            qpad_ref[gg * HEAD_DIM:(gg + 1) * HEAD_DIM, h * TQ:(h + 1) * TQ] = (
                qt[h * HEAD_DIM:(h + 1) * HEAD_DIM, :] if gg == g else zero_half)

    row = lax.broadcasted_iota(jnp.int32, (TK, TQ), 0)
    col = lax.broadcasted_iota(jnp.int32, (TK, TQ), 1)
    row_minus_col = row - col

    def score_chunk(c):
        kic = ki_ref[0, c]
        lg_all = jnp.dot(kic, qit_ref[...], preferred_element_type=F32)
        acc = jnp.zeros((TK, TQ), F32)
        for h in range(IDX_HEADS):
            lg = lg_all[:, h * TQ:(h + 1) * TQ]
            acc = acc + wt[h:h + 1, :] * jnp.maximum(lg, 0.0)
        noncausal = row_minus_col > (i - c) * TQ
        score_ref[c] = jnp.where(noncausal, NEG_INF, acc * IDX_SCALE)

    def score_quad(j, carry):
        for t in range(4):
            score_chunk(4 * j + t)
        return carry

    lax.fori_loop(0, nk // 4, score_quad, 0)

    @pl.when(nk % 4 >= 2)
    def _():
        score_chunk(nk - nk % 4)
        score_chunk(nk - nk % 4 + 1)

    @pl.when(nk % 2 == 1)
    def _():
        score_chunk(nk - 1)

    def count_where(pred, thr):
        thr_ref[...] = jnp.broadcast_to(thr, (SUBLANES, TQ))
        thr_g = [thr_ref[:, g * LANES:(g + 1) * LANES] for g in range(n_lane_groups)]

        def chunk_counts(c, accs, diagonal):
            accs = [list(a) for a in accs]
            for r in range(n_groups):
                for g in range(n_lane_groups):
                    if diagonal and r * SUBLANES >= (g + 1) * LANES:
                        continue
                    blk = score_ref[c, r * SUBLANES:(r + 1) * SUBLANES, g * LANES:(g + 1) * LANES]
                    a = accs[g][r % COUNT_CHAINS]
                    accs[g][r % COUNT_CHAINS] = jnp.where(pred(blk, thr_g[g]), a + 1.0, a)
            return tuple(tuple(a) for a in accs)

        zero = jnp.zeros((SUBLANES, LANES), F32)
        accs = ((zero,) * COUNT_CHAINS,) * n_lane_groups
        n_full = nk - 1
        accs = lax.fori_loop(
            0, n_full // 2,
            lambda j, a: chunk_counts(2 * j + 1, chunk_counts(2 * j, a, False), False), accs)
        accs = lax.cond(n_full % 2 == 1, lambda a: chunk_counts(n_full - 1, a, False), lambda a: a, accs)
        accs = chunk_counts(nk - 1, accs, True)
        per_group = [sum(a[1:], a[0]) for a in accs]
        return jnp.sum(jnp.concatenate(per_group, axis=1), axis=0, keepdims=True)

    def count_ge(cand_key):
        return count_where(lambda blk, thr: blk >= thr, _sortable_to_f32(cand_key))

    def bit_body(b, carry):
        prefix, cntp = carry
        cand_key = prefix | jnp.left_shift(jnp.int32(1), KEY_BITS - 2 - b)
        cnt = count_ge(cand_key)
        ok = cnt >= kf
        return jnp.where(ok, cand_key, prefix), jnp.where(ok, cnt, cntp)

    tau_ref[...] = jnp.full(tau_ref.shape, -F32_MAX, F32)
    cntp_ref[...] = jnp.zeros(cntp_ref.shape, F32)

    @pl.when(nk * TK > k_sel)
    def _():
        cnt0 = count_ge(jnp.zeros((1, TQ), jnp.int32))
        ok0 = cnt0 >= kf
        prefix0 = jnp.where(ok0, jnp.int32(0), jnp.int32(INT_MIN))
        prefix, cntp = lax.fori_loop(0, KEY_BITS - 1, bit_body, (prefix0, jnp.where(ok0, cnt0, 0.0)))
        tau_found = jnp.where(prefix == INT_MIN, -F32_MAX, _sortable_to_f32(prefix))
        tau_ref[...] = jnp.broadcast_to(tau_found, tau_ref.shape)
        cntp_ref[...] = jnp.broadcast_to(cntp, cntp_ref.shape)

    tau = tau_ref[0:1, :]
    cntp = cntp_ref[0:1, :]

    def bias_body(c, carry):
        bias_ref[c] = jnp.where(score_ref[c] >= tau, 0.0, NEG_INF)
        return carry

    lax.fori_loop(0, nk, bias_body, 0)

    @pl.when(jnp.max(cntp) > kf)
    def _():
        need = kf - count_where(lambda blk, thr: blk > thr, tau)
        r = lax.broadcasted_iota(jnp.int32, (TK, TK), 0)
        cc = lax.broadcasted_iota(jnp.int32, (TK, TK), 1)
        before = jnp.where(cc < r, 1.0, 0.0).astype(BF16)

        def tie_body(c, seen):
            sc = score_ref[c]
            eq = sc == tau
            eqf = jnp.where(eq, 1.0, 0.0)
            rank = jnp.dot(before, eqf.astype(BF16), preferred_element_type=F32) + seen
            keep = (sc > tau) | (eq & (rank < need))
            bias_ref[c] = jnp.where(keep, 0.0, NEG_INF)
            return seen + jnp.sum(eqf, axis=0, keepdims=True)

        lax.fori_loop(0, nk, tie_body, jnp.zeros((1, TQ), F32))

    m_ref[...] = jnp.full(m_ref.shape, NEG_INF, F32)
    l_ref[...] = jnp.zeros(l_ref.shape, F32)
    acc_ref[...] = jnp.zeros(acc_ref.shape, F32)
    alpha_ref[...] = jnp.ones(alpha_ref.shape, F32)
    p1_ref[...] = jnp.zeros(p1_ref.shape, BF16)
    ones_rows = jnp.ones((BF16_SUBLANES, TK), BF16)

    def logits_head(h, kc, bias, s_out, mc_out):
        cols = slice(h * TQ, (h + 1) * TQ)
        s = jnp.dot(kc, qpad_ref[:, cols], preferred_element_type=F32) + bias
        s_out[:, cols] = s
        mc_out[h] = jnp.broadcast_to(jnp.max(s, axis=0, keepdims=True), (SUBLANES, TQ))

    def pv_update(c, p_in):
        for g in range(N_KV_HEADS):
            lanes = slice(g * rep_heads * TQ, (g + 1) * rep_heads * TQ)
            lhs = jnp.concatenate([vt_ref[0, c, g * HEAD_DIM:(g + 1) * HEAD_DIM, :], ones_rows], axis=0)
            pv = jnp.dot(lhs, p_in[:, lanes], preferred_element_type=F32)
            for hh in range(rep_heads):
                h = g * rep_heads + hh
                rows = slice(h * HEAD_DIM, (h + 1) * HEAD_DIM)
                cols = slice(hh * TQ, (hh + 1) * TQ)
                alpha = alpha_ref[h, 0:1, :]
                acc_ref[rows, :] = alpha * acc_ref[rows, :] + pv[:HEAD_DIM, cols]
                l_ref[h] = jnp.broadcast_to(alpha * l_ref[h, 0:1, :] + pv[HEAD_DIM:HEAD_DIM + 1, cols],
                                            (SUBLANES, TQ))

    def softmax_head(h, s_in, mc_in, p_out):
        m_old = m_ref[h, 0:1, :]
        m_new = jnp.maximum(m_old, mc_in[h, 0:1, :])
        m_use = jnp.where(m_new == NEG_INF, 0.0, m_new)
        alpha_ref[h] = jnp.broadcast_to(jnp.exp2(m_old - m_use), (SUBLANES, TQ))
        m_ref[h] = jnp.broadcast_to(m_new, (SUBLANES, TQ))
        cols = slice(h * TQ, (h + 1) * TQ)
        for blk in range(TK // EXP_ROWS):
            rows = slice(blk * EXP_ROWS, (blk + 1) * EXP_ROWS)
            p_out[rows, cols] = jnp.exp2(s_in[rows, cols] - m_use).astype(BF16)

    def stage(c, s_cur, mc_cur, p_cur, s_nxt, mc_nxt, p_prev):
        c_nxt = jnp.minimum(c + 1, nk - 1)
        pv_update(jnp.maximum(c - 1, 0), p_prev)
        bias_nxt = bias_ref[c_nxt]
        k_nxt = k_ref[0, c_nxt]
        for h in range(N_HEADS):
            logits_head(h, k_nxt, bias_nxt, s_nxt, mc_nxt)
            softmax_head(h, s_cur, mc_cur, p_cur)

    for h in range(N_HEADS):
        logits_head(h, k_ref[0, 0], bias_ref[0], s0_ref, mc0_ref)

    def pair_body(j, carry):
        c = 2 * j
        stage(c, s0_ref, mc0_ref, p0_ref, s1_ref, mc1_ref, p1_ref)

        @pl.when(c + 1 < nk)
        def _():
            stage(c + 1, s1_ref, mc1_ref, p1_ref, s0_ref, mc0_ref, p0_ref)

        return carry

    lax.fori_loop(0, (nk + 1) // 2, pair_body, 0)

    @pl.when((nk - 1) % 2 == 0)
    def _():
        pv_update(nk - 1, p0_ref)

    @pl.when((nk - 1) % 2 == 1)
    def _():
        pv_update(nk - 1, p1_ref)

    for h in range(N_HEADS):
        rows = slice(h * HEAD_DIM, (h + 1) * HEAD_DIM)
        acc_ref[rows, :] = acc_ref[rows, :] / l_ref[h, 0:1, :]
    yb_ref[0] = acc_ref[...].T.astype(BF16)


def _run_attn(qi, wi, ki, q, k, vt, k_sel):
    b, s, _ = q.shape
    nc = s // TK
    ki = ki.reshape(b, nc, TK, IDX_DIM)
    k = k.reshape(b, nc, TK, D_KV)
    rows = lambda width: pl.BlockSpec((1, TQ, width), lambda bi, i: (bi, i, 0))
    keys = lambda d0, d1: pl.BlockSpec((1, nc, d0, d1), lambda bi, i: (bi, 0, 0, 0))
    return pl.pallas_call(
        functools.partial(_attn_kernel, k_sel=k_sel),
        grid=(b, s // TQ),
        in_specs=[rows(IDX_HEADS * IDX_DIM), rows(LANES), keys(TK, IDX_DIM),
                  rows(D_ATTN), keys(TK, D_KV), keys(D_KV, TK)],
        out_specs=rows(D_ATTN),
        out_shape=jax.ShapeDtypeStruct((b, s, D_ATTN), BF16),
        scratch_shapes=[pltpu.VMEM((nc, TK, TQ), F32),
                        pltpu.VMEM((nc, TK, TQ), F32),
                        pltpu.VMEM((IDX_DIM, IDX_HEADS * TQ), BF16),
                        pltpu.VMEM((D_KV, N_HEADS * TQ), BF16),
                        pltpu.VMEM((TK, N_HEADS * TQ), F32),
                        pltpu.VMEM((TK, N_HEADS * TQ), F32),
                        pltpu.VMEM((TK, N_HEADS * TQ), BF16),
                        pltpu.VMEM((TK, N_HEADS * TQ), BF16),
                        pltpu.VMEM((N_HEADS, SUBLANES, TQ), F32),
                        pltpu.VMEM((N_HEADS, SUBLANES, TQ), F32),
                        pltpu.VMEM((N_HEADS, SUBLANES, TQ), F32),
                        pltpu.VMEM((N_HEADS, SUBLANES, TQ), F32),
                        pltpu.VMEM((N_HEADS, SUBLANES, TQ), F32),
                        pltpu.VMEM((D_ATTN, TQ), F32),
                        pltpu.VMEM((SUBLANES, TQ), F32),
                        pltpu.VMEM((SUBLANES, TQ), F32),
                        pltpu.VMEM((SUBLANES, TQ), F32)],
        compiler_params=pltpu.CompilerParams(
            dimension_semantics=("parallel", "parallel"), vmem_limit_bytes=VMEM_LIMIT),
        name="dsa_attention",
    )(qi, wi, ki, q, k, vt)


def _tail_kernel(x_ref, ya_ref, yb_ref, wg_ref, wbr_ref, wo_ref, wup_ref, wdn_ref,
                 g1_ref, b1_ref, g2_ref, b2_ref, out_ref, *, alpha):
    part = TM_TAIL // TAIL_PARTS
    parts = [slice(p * part, (p + 1) * part) for p in range(TAIL_PARTS)]

    def mix_pre_ln(rows):
        x = x_ref[rows, :]
        xb = x.astype(BF16)
        merged = jnp.zeros((part, D_MODEL), F32)
        for n, y_ref in enumerate((ya_ref, yb_ref)):
            gate = _dot_nt(xb, wg_ref[n * D_MODEL:(n + 1) * D_MODEL, :])
            branch = jnp.dot(y_ref[rows, :], wbr_ref[n], preferred_element_type=F32)
            merged = merged + jax.nn.sigmoid(gate) * branch
        return alpha * x + jnp.dot(merged.astype(BF16), wo_ref[...], preferred_element_type=F32)

    def mlp_pre_ln(h):
        hb = h.astype(BF16)
        ff = jnp.zeros((part, D_MODEL), F32)
        for f in range(D_FF // FF_CHUNK):
            up = jnp.dot(hb, wup_ref[:, f * FF_CHUNK:(f + 1) * FF_CHUNK], preferred_element_type=F32)
            act = jnp.square(jnp.maximum(up, 0.0)).astype(BF16)
            ff = ff + jnp.dot(act, wdn_ref[f * FF_CHUNK:(f + 1) * FF_CHUNK, :], preferred_element_type=F32)
        return alpha * h + ff

    ln1 = lambda v: _layer_norm_rows(v, g1_ref[...], b1_ref[...])
    ln2 = lambda v: _layer_norm_rows(v, g2_ref[...], b2_ref[...])
    hs = [ln1(mix_pre_ln(rows)) for rows in parts]
    for rows, h in zip(parts, hs):
        out_ref[rows, :] = ln2(mlp_pre_ln(h))


def _run_tail(x2, ya2, yb2, wg, wbr, wo, wup, wdn, g1, b1, g2, b2, alpha):
    n, d = x2.shape
    const = lambda *shape: pl.BlockSpec(shape, lambda r: (0,) * len(shape),
                                        pipeline_mode=pl.Buffered(1))
    rows = lambda width: pl.BlockSpec((TM_TAIL, width), lambda r: (r, 0))
    return pl.pallas_call(
        functools.partial(_tail_kernel, alpha=alpha),
        grid=(n // TM_TAIL,),
        in_specs=[rows(d), rows(D_CONV), rows(D_ATTN),
                  const(N_BRANCH * d, d), const(N_BRANCH, D_CONV, d), const(d, d),
                  const(d, D_FF), const(D_FF, d),
                  const(1, d), const(1, d), const(1, d), const(1, d)],
        out_specs=rows(d),
        out_shape=jax.ShapeDtypeStruct((n, d), F32),
        compiler_params=pltpu.CompilerParams(
            dimension_semantics=("parallel",), vmem_limit_bytes=VMEM_LIMIT),
        name="merge_mlp",
    )(x2, ya2, yb2, wg, wbr, wo, wup, wdn, g1, b1, g2, b2)


def kernel(x, w_in, conv_w, idx_k_norm_g, idx_k_norm_b, w_branch, w_o, ln1_g, ln1_b, w_up, w_down, ln2_g, ln2_b):
    b, s, d = x.shape
    depth = w_in.shape[0]
    assert d == D_MODEL and s % TM_PROJ == 0 and (b * s) % TM_TAIL == 0 and TQ == TK
    k_sel = min(TOPK_MAX, s // 4)
    alpha = (2 * depth) ** 0.25
    for l in range(depth):
        wt = jnp.swapaxes(w_in[l], 0, 1)
        pad = lambda a: jnp.pad(a.astype(F32), (0, LANES - IDX_DIM)).reshape(1, LANES)
        later_weights = (w_branch[l].reshape(N_BRANCH * D_CONV, d), w_o[l], w_up[l], w_down[l])
        ya, q, k, vt, qi, ki, wi, wg, wbr, wo, wup, wdn = _run_proj(
            x, wt, conv_w[l], pad(idx_k_norm_g[l]), pad(idx_k_norm_b[l]), later_weights)
        yb = _run_attn(qi, wi, ki, q, k, vt, k_sel)
        row = lambda a: a.astype(F32).reshape(1, d)
        out = _run_tail(x.reshape(b * s, d), ya.reshape(b * s, D_CONV), yb.reshape(b * s, D_ATTN),
                        wg, wbr.reshape(N_BRANCH, D_CONV, d), wo, wup, wdn,
                        row(ln1_g[l]), row(ln1_b[l]), row(ln2_g[l]), row(ln2_b[l]), alpha)
        x = out.reshape(b, s, d)
    return x
```

```python
import functools

import jax
import jax.numpy as jnp
from jax import lax
from jax.experimental import pallas as pl
from jax.experimental.pallas import tpu as pltpu

D_MODEL = 1024
D_CONV = 512
CONV_WIDTH = 3
N_HEADS = 8
N_KV_HEADS = 2
HEAD_DIM = 64
D_ATTN = N_HEADS * HEAD_DIM
D_KV = N_KV_HEADS * HEAD_DIM
IDX_HEADS = 8
IDX_DIM = 64
TOPK_MAX = 256
N_BRANCH = 2
D_FF = 4 * D_MODEL
LN_EPS = 1e-5
IDX_SCALE = (IDX_DIM ** -0.5) * (IDX_HEADS ** -0.5)
ATTN_SCALE = HEAD_DIM ** -0.5
LOG2_E = 1.4426950408889634
Q_SCALE = ATTN_SCALE * LOG2_E

OFF_Q = 3 * D_CONV
OFF_K = OFF_Q + D_ATTN
OFF_V = OFF_K + D_KV
OFF_QI = OFF_V + D_KV
OFF_KI = OFF_QI + IDX_HEADS * IDX_DIM
OFF_WI = OFF_KI + IDX_DIM
OFF_GATES = OFF_WI + IDX_HEADS

LANES = 128
SUBLANES = 8
BF16_SUBLANES = 16
VMEM_LIMIT = 56 * 1024 * 1024

TM_PROJ = 1024
TQ = 256
TK = 256
TM_TAIL = 512
TAIL_PARTS = 2
FF_CHUNK = 1024
COUNT_CHAINS = 4
EXP_ROWS = 16

F32 = jnp.float32
BF16 = jnp.bfloat16
NEG_INF = float("-inf")
F32_MAX = float(jnp.finfo(jnp.float32).max)
INT_MIN = -2 ** 31
KEY_BITS = 32
CAST_ROWS = (N_BRANCH * D_CONV, D_MODEL, D_MODEL, D_FF)


def _dot_nt(a, b_t):
    return lax.dot_general(a, b_t, (((1,), (1,)), ((), ())), preferred_element_type=F32)


def _layer_norm_rows(x, g, b):
    mu = jnp.mean(x, axis=-1, keepdims=True)
    xc = x - mu
    var = jnp.mean(xc * xc, axis=-1, keepdims=True)
    return xc * lax.rsqrt(var + LN_EPS) * g + b


def _proj_kernel(x_ref, wt_ref, convw_ref, kg_ref, kb_ref, *rest):
    n_cast = len(CAST_ROWS)
    cast_in, rest = rest[:n_cast], rest[n_cast:]
    ya_ref, q_ref, k_ref, vt_ref, qi_ref, ki_ref, wi_ref, wg_ref = rest[:8]
    cast_out, carry_ref = rest[8:8 + n_cast], rest[8 + n_cast]
    for src, dst in zip(cast_in, cast_out):
        dst[...] = src[...].astype(BF16)
    step = pl.program_id(0) * pl.num_programs(1) + pl.program_id(1)
    gate_rows = wg_ref.shape[0]
    wg_ref[...] = wt_ref[pl.ds(pl.multiple_of(OFF_GATES + step * gate_rows, SUBLANES), gate_rows), :].astype(BF16)

    @pl.when(pl.program_id(1) == 0)
    def _():
        carry_ref[...] = jnp.zeros_like(carry_ref)

    xb = x_ref[0].astype(BF16)

    def proj(lo, hi):
        return _dot_nt(xb, wt_ref[lo:hi, :].astype(BF16))

    zk = _dot_nt(xb, wt_ref[OFF_KI:OFF_KI + LANES, :].astype(BF16))
    wi_ref[0] = zk
    valid = lax.broadcasted_iota(jnp.int32, zk.shape, 1) < IDX_DIM
    mu = jnp.sum(jnp.where(valid, zk, 0.0), axis=-1, keepdims=True) * (1.0 / IDX_DIM)
    xc = jnp.where(valid, zk - mu, 0.0)
    var = jnp.sum(xc * xc, axis=-1, keepdims=True) * (1.0 / IDX_DIM)
    kin = xc * lax.rsqrt(var + LN_EPS) * kg_ref[...] + kb_ref[...]
    ki_ref[0] = kin[:, :IDX_DIM].astype(BF16)

    vt = proj(OFF_V, OFF_QI).T
    for c in range(TM_PROJ // TK):
        vt_ref[0, c] = vt[:, c * TK:(c + 1) * TK].astype(BF16)

    cu = proj(D_CONV, 2 * D_CONV) * proj(2 * D_CONV, 3 * D_CONV)
    prev = carry_ref[...]
    row = lax.broadcasted_iota(jnp.int32, cu.shape, 0)
    cu1 = jnp.where(row == 0, prev[SUBLANES - 1:SUBLANES, :], pltpu.roll(cu, 1, axis=0))
    cu2 = jnp.where(row == 0, prev[SUBLANES - 2:SUBLANES - 1, :],
                    jnp.where(row == 1, prev[SUBLANES - 1:SUBLANES, :], pltpu.roll(cu, 2, axis=0)))
    carry_ref[...] = cu[TM_PROJ - SUBLANES:, :]
    cw = convw_ref[...]
    conv = cw[0:1, :] * cu2 + cw[1:2, :] * cu1 + cw[2:3, :] * cu
    ya_ref[0] = (proj(0, D_CONV) * conv).astype(BF16)

    k_ref[0] = proj(OFF_K, OFF_V).astype(BF16)
    q_ref[0] = (proj(OFF_Q, OFF_K) * Q_SCALE).astype(BF16)
    qi_ref[0] = proj(OFF_QI, OFF_KI).astype(BF16)


def _run_proj(x, wt, conv_w, kg, kb, cast_weights):
    b, s, d = x.shape
    nc = s // TK
    cpt = TM_PROJ // TK
    steps = s // TM_PROJ
    n_steps = b * steps
    assert tuple(w.shape[0] for w in cast_weights) == CAST_ROWS
    gate_rows = wt.shape[0] - OFF_GATES
    cast_specs = [pl.BlockSpec((w.shape[0] // n_steps, w.shape[1]), lambda bi, j: (bi * steps + j, 0))
                  for w in cast_weights]
    const = lambda *shape: pl.BlockSpec(shape, lambda bi, j: (0,) * len(shape),
                                        pipeline_mode=pl.Buffered(1))
    rows = lambda width: pl.BlockSpec((1, TM_PROJ, width), lambda bi, j: (bi, j, 0))
    return pl.pallas_call(
        _proj_kernel,
        grid=(b, s // TM_PROJ),
        in_specs=[rows(d), const(*wt.shape), const(CONV_WIDTH, D_CONV),
                  const(1, LANES), const(1, LANES)] + cast_specs,
        out_specs=[rows(D_CONV), rows(D_ATTN), rows(D_KV),
                   pl.BlockSpec((1, cpt, D_KV, TK), lambda bi, j: (bi, j, 0, 0)),
                   rows(IDX_HEADS * IDX_DIM), rows(IDX_DIM), rows(LANES),
                   pl.BlockSpec((gate_rows // n_steps, d), lambda bi, j: (bi * steps + j, 0))] + cast_specs,
        out_shape=[jax.ShapeDtypeStruct((b, s, D_CONV), BF16),
                   jax.ShapeDtypeStruct((b, s, D_ATTN), BF16),
                   jax.ShapeDtypeStruct((b, s, D_KV), BF16),
                   jax.ShapeDtypeStruct((b, nc, D_KV, TK), BF16),
                   jax.ShapeDtypeStruct((b, s, IDX_HEADS * IDX_DIM), BF16),
                   jax.ShapeDtypeStruct((b, s, IDX_DIM), BF16),
                   jax.ShapeDtypeStruct((b, s, LANES), F32),
                   jax.ShapeDtypeStruct((gate_rows, d), BF16)]
        + [jax.ShapeDtypeStruct(w.shape, BF16) for w in cast_weights],
        scratch_shapes=[pltpu.VMEM((SUBLANES, D_CONV), F32)],
        compiler_params=pltpu.CompilerParams(
            dimension_semantics=("parallel", "arbitrary"), vmem_limit_bytes=VMEM_LIMIT),
        name="proj_mixer_a",
    )(x, wt, conv_w, kg, kb, *cast_weights)


def _sortable_to_f32(x):
    bits = x ^ ((x >> 31) & jnp.int32(0x7FFFFFFF))
    return pltpu.bitcast(bits, F32)


def _attn_kernel(qi_ref, wi_ref, ki_ref, q_ref, k_ref, vt_ref, yb_ref,
                 score_ref, bias_ref, qit_ref, qpad_ref, s0_ref, s1_ref, p0_ref, p1_ref,
                 mc0_ref, mc1_ref, alpha_ref, m_ref, l_ref, acc_ref, thr_ref, tau_ref, cntp_ref, *, k_sel):
    i = pl.program_id(1)
    nk = i + 1
    kf = float(k_sel)
    rep_heads = N_HEADS // N_KV_HEADS
    n_groups = TK // SUBLANES
    n_lane_groups = TQ // LANES

    qit = qi_ref[0].astype(F32).T.astype(BF16)
    for h in range(IDX_HEADS):
        qit_ref[:, h * TQ:(h + 1) * TQ] = qit[h * IDX_DIM:(h + 1) * IDX_DIM, :]
    wt = wi_ref[0].T[IDX_DIM:IDX_DIM + IDX_HEADS, :]
    qt = q_ref[0].astype(F32).T.astype(BF16)
    zero_half = jnp.zeros((HEAD_DIM, TQ), BF16)
    for h in range(N_HEADS):
        g = h // rep_heads
        for gg in range(N_KV_HEADS):
            qpad_ref[gg * HEAD_DIM:(gg + 1) * HEAD_DIM, h * TQ:(h + 1) * TQ] = (
                qt[h * HEAD_DIM:(h + 1) * HEAD_DIM, :] if gg == g else zero_half)

    row = lax.broadcasted_iota(jnp.int32, (TK, TQ), 0)
    col = lax.broadcasted_iota(jnp.int32, (TK, TQ), 1)
    row_minus_col = row - col

    def score_chunk(c):
        kic = ki_ref[0, c]
        lg_all = jnp.dot(kic, qit_ref[...], preferred_element_type=F32)
        acc = jnp.zeros((TK, TQ), F32)
        for h in range(IDX_HEADS):
            lg = lg_all[:, h * TQ:(h + 1) * TQ]
            acc = acc + wt[h:h + 1, :] * jnp.maximum(lg, 0.0)
        noncausal = row_minus_col > (i - c) * TQ
        score_ref[c] = jnp.where(noncausal, NEG_INF, acc * IDX_SCALE)

    def score_quad(j, carry):
        for t in range(4):
            score_chunk(4 * j + t)
        return carry

    lax.fori_loop(0, nk // 4, score_quad, 0)

    @pl.when(nk % 4 >= 2)
    def _():
        score_chunk(nk - nk % 4)
        score_chunk(nk - nk % 4 + 1)

    @pl.when(nk % 2 == 1)
    def _():
        score_chunk(nk - 1)

    def count_where(pred, thr):
        thr_ref[...] = jnp.broadcast_to(thr, (SUBLANES, TQ))
        thr_g = [thr_ref[:, g * LANES:(g + 1) * LANES] for g in range(n_lane_groups)]

        def chunk_counts(c, accs, diagonal):
            accs = [list(a) for a in accs]
            for r in range(n_groups):
                for g in range(n_lane_groups):
                    if diagonal and r * SUBLANES >= (g + 1) * LANES:
                        continue
                    blk = score_ref[c, r * SUBLANES:(r + 1) * SUBLANES, g * LANES:(g + 1) * LANES]
                    a = accs[g][r % COUNT_CHAINS]
                    accs[g][r % COUNT_CHAINS] = jnp.where(pred(blk, thr_g[g]), a + 1.0, a)
            return tuple(tuple(a) for a in accs)

        zero = jnp.zeros((SUBLANES, LANES), F32)
        accs = ((zero,) * COUNT_CHAINS,) * n_lane_groups
        n_full = nk - 1
        accs = lax.fori_loop(
            0, n_full // 2,
            lambda j, a: chunk_counts(2 * j + 1, chunk_counts(2 * j, a, False), False), accs)
        accs = lax.cond(n_full % 2 == 1, lambda a: chunk_counts(n_full - 1, a, False), lambda a: a, accs)
        accs = chunk_counts(nk - 1, accs, True)
        per_group = [sum(a[1:], a[0]) for a in accs]
        return jnp.sum(jnp.concatenate(per_group, axis=1), axis=0, keepdims=True)

    def count_ge(cand_key):
        return count_where(lambda blk, thr: blk >= thr, _sortable_to_f32(cand_key))

    def bit_body(b, carry):
        prefix, cntp = carry
        cand_key = prefix | jnp.left_shift(jnp.int32(1), KEY_BITS - 2 - b)
        cnt = count_ge(cand_key)
        ok = cnt >= kf
        return jnp.where(ok, cand_key, prefix), jnp.where(ok, cnt, cntp)

    tau_ref[...] = jnp.full(tau_ref.shape, -F32_MAX, F32)
    cntp_ref[...] = jnp.zeros(cntp_ref.shape, F32)

    @pl.when(nk * TK > k_sel)
    def _():
        cnt0 = count_ge(jnp.zeros((1, TQ), jnp.int32))
        ok0 = cnt0 >= kf
        prefix0 = jnp.where(ok0, jnp.int32(0), jnp.int32(INT_MIN))
        prefix, cntp = lax.fori_loop(0, KEY_BITS - 1, bit_body, (prefix0, jnp.where(ok0, cnt0, 0.0)))
        tau_found = jnp.where(prefix == INT_MIN, -F32_MAX, _sortable_to_f32(prefix))
        tau_ref[...] = jnp.broadcast_to(tau_found, tau_ref.shape)
        cntp_ref[...] = jnp.broadcast_to(cntp, cntp_ref.shape)

    tau = tau_ref[0:1, :]
    cntp = cntp_ref[0:1, :]

    def bias_body(c, carry):
        bias_ref[c] = jnp.where(score_ref[c] >= tau, 0.0, NEG_INF)
        return carry

    lax.fori_loop(0, nk, bias_body, 0)

    @pl.when(jnp.max(cntp) > kf)
    def _():
        need = kf - count_where(lambda blk, thr: blk > thr, tau)
        r = lax.broadcasted_iota(jnp.int32, (TK, TK), 0)
        cc = lax.broadcasted_iota(jnp.int32, (TK, TK), 1)
        before = jnp.where(cc < r, 1.0, 0.0).astype(BF16)

        def tie_body(c, seen):
            sc = score_ref[c]
            eq = sc == tau
            eqf = jnp.where(eq, 1.0, 0.0)
            rank = jnp.dot(before, eqf.astype(BF16), preferred_element_type=F32) + seen
            keep = (sc > tau) | (eq & (rank < need))
            bias_ref[c] = jnp.where(keep, 0.0, NEG_INF)
            return seen + jnp.sum(eqf, axis=0, keepdims=True)

        lax.fori_loop(0, nk, tie_body, jnp.zeros((1, TQ), F32))

    m_ref[...] = jnp.full(m_ref.shape, NEG_INF, F32)
    l_ref[...] = jnp.zeros(l_ref.shape, F32)
    acc_ref[...] = jnp.zeros(acc_ref.shape, F32)
    alpha_ref[...] = jnp.ones(alpha_ref.shape, F32)
    p1_ref[...] = jnp.zeros(p1_ref.shape, BF16)
    ones_rows = jnp.ones((BF16_SUBLANES, TK), BF16)

    def logits_head(h, kc, bias, s_out, mc_out):
        cols = slice(h * TQ, (h + 1) * TQ)
        s = jnp.dot(kc, qpad_ref[:, cols], preferred_element_type=F32) + bias
        s_out[:, cols] = s
        mc_out[h] = jnp.broadcast_to(jnp.max(s, axis=0, keepdims=True), (SUBLANES, TQ))

    def pv_update(c, p_in):
        for g in range(N_KV_HEADS):
            lanes = slice(g * rep_heads * TQ, (g + 1) * rep_heads * TQ)
            lhs = jnp.concatenate([vt_ref[0, c, g * HEAD_DIM:(g + 1) * HEAD_DIM, :], ones_rows], axis=0)
            pv = jnp.dot(lhs, p_in[:, lanes], preferred_element_type=F32)
            for hh in range(rep_heads):
                h = g * rep_heads + hh
                rows = slice(h * HEAD_DIM, (h + 1) * HEAD_DIM)
                cols = slice(hh * TQ, (hh + 1) * TQ)
                alpha = alpha_ref[h, 0:1, :]
                acc_ref[rows, :] = alpha * acc_ref[rows, :] + pv[:HEAD_DIM, cols]
                l_ref[h] = jnp.broadcast_to(alpha * l_ref[h, 0:1, :] + pv[HEAD_DIM:HEAD_DIM + 1, cols],
                                            (SUBLANES, TQ))

    def softmax_head(h, s_in, mc_in, p_out):
        m_old = m_ref[h, 0:1, :]
        m_new = jnp.maximum(m_old, mc_in[h, 0:1, :])
        m_use = jnp.where(m_new == NEG_INF, 0.0, m_new)
        alpha_ref[h] = jnp.broadcast_to(jnp.exp2(m_old - m_use), (SUBLANES, TQ))
        m_ref[h] = jnp.broadcast_to(m_new, (SUBLANES, TQ))
        cols = slice(h * TQ, (h + 1) * TQ)
        for blk in range(TK // EXP_ROWS):
            rows = slice(blk * EXP_ROWS, (blk + 1) * EXP_ROWS)
            p_out[rows, cols] = jnp.exp2(s_in[rows, cols] - m_use).astype(BF16)

    def stage(c, s_cur, mc_cur, p_cur, s_nxt, mc_nxt, p_prev):
        c_nxt = jnp.minimum(c + 1, nk - 1)
        pv_update(jnp.maximum(c - 1, 0), p_prev)
        bias_nxt = bias_ref[c_nxt]
        k_nxt = k_ref[0, c_nxt]
        for h in range(N_HEADS):
            logits_head(h, k_nxt, bias_nxt, s_nxt, mc_nxt)
            softmax_head(h, s_cur, mc_cur, p_cur)

    for h in range(N_HEADS):
        logits_head(h, k_ref[0, 0], bias_ref[0], s0_ref, mc0_ref)

    def pair_body(j, carry):
        c = 2 * j
        stage(c, s0_ref, mc0_ref, p0_ref, s1_ref, mc1_ref, p1_ref)

        @pl.when(c + 1 < nk)
        def _():
            stage(c + 1, s1_ref, mc1_ref, p1_ref, s0_ref, mc0_ref, p0_ref)

        return carry

    lax.fori_loop(0, (nk + 1) // 2, pair_body, 0)

    @pl.when((nk - 1) % 2 == 0)
    def _():
        pv_update(nk - 1, p0_ref)

    @pl.when((nk - 1) % 2 == 1)
    def _():
        pv_update(nk - 1, p1_ref)

    for h in range(N_HEADS):
        rows = slice(h * HEAD_DIM, (h + 1) * HEAD_DIM)
        acc_ref[rows, :] = acc_ref[rows, :] / l_ref[h, 0:1, :]
    yb_ref[0] = acc_ref[...].T.astype(BF16)


def _run_attn(qi, wi, ki, q, k, vt, k_sel):
    b, s, _ = q.shape
    nc = s // TK
    ki = ki.reshape(b, nc, TK, IDX_DIM)
    k = k.reshape(b, nc, TK, D_KV)
    rows = lambda width: pl.BlockSpec((1, TQ, width), lambda bi, i: (bi, i, 0))
    keys = lambda d0, d1: pl.BlockSpec((1, nc, d0, d1), lambda bi, i: (bi, 0, 0, 0))
    return pl.pallas_call(
        functools.partial(_attn_kernel, k_sel=k_sel),
        grid=(b, s // TQ),
        in_specs=[rows(IDX_HEADS * IDX_DIM), rows(LANES), keys(TK, IDX_DIM),
                  rows(D_ATTN), keys(TK, D_KV), keys(D_KV, TK)],
        out_specs=rows(D_ATTN),
        out_shape=jax.ShapeDtypeStruct((b, s, D_ATTN), BF16),
        scratch_shapes=[pltpu.VMEM((nc, TK, TQ), F32),
                        pltpu.VMEM((nc, TK, TQ), F32),
                        pltpu.VMEM((IDX_DIM, IDX_HEADS * TQ), BF16),
                        pltpu.VMEM((D_KV, N_HEADS * TQ), BF16),
                        pltpu.VMEM((TK, N_HEADS * TQ), F32),
                        pltpu.VMEM((TK, N_HEADS * TQ), F32),
                        pltpu.VMEM((TK, N_HEADS * TQ), BF16),
                        pltpu.VMEM((TK, N_HEADS * TQ), BF16),
                        pltpu.VMEM((N_HEADS, SUBLANES, TQ), F32),
                        pltpu.VMEM((N_HEADS, SUBLANES, TQ), F32),
                        pltpu.VMEM((N_HEADS, SUBLANES, TQ), F32),
                        pltpu.VMEM((N_HEADS, SUBLANES, TQ), F32),
                        pltpu.VMEM((N_HEADS, SUBLANES, TQ), F32),
                        pltpu.VMEM((D_ATTN, TQ), F32),
                        pltpu.VMEM((SUBLANES, TQ), F32),
                        pltpu.VMEM((SUBLANES, TQ), F32),
                        pltpu.VMEM((SUBLANES, TQ), F32)],
        compiler_params=pltpu.CompilerParams(
            dimension_semantics=("parallel", "parallel"), vmem_limit_bytes=VMEM_LIMIT),
        name="dsa_attention",
    )(qi, wi, ki, q, k, vt)


def _tail_kernel(x_ref, ya_ref, yb_ref, wg_ref, wbr_ref, wo_ref, wup_ref, wdn_ref,
                 g1_ref, b1_ref, g2_ref, b2_ref, out_ref, *, alpha):
    part = TM_TAIL // TAIL_PARTS
    parts = [slice(p * part, (p + 1) * part) for p in range(TAIL_PARTS)]

    def mix_pre_ln(rows):
        x = x_ref[rows, :]
        xb = x.astype(BF16)
        merged = jnp.zeros((part, D_MODEL), F32)
        for n, y_ref in enumerate((ya_ref, yb_ref)):
            gate = _dot_nt(xb, wg_ref[n * D_MODEL:(n + 1) * D_MODEL, :])
            branch = jnp.dot(y_ref[rows, :], wbr_ref[n], preferred_element_type=F32)
            merged = merged + jax.nn.sigmoid(gate) * branch
        return alpha * x + jnp.dot(merged.astype(BF16), wo_ref[...], preferred_element_type=F32)

    def mlp_pre_ln(h):
        hb = h.astype(BF16)
        ff = jnp.zeros((part, D_MODEL), F32)
        for f in range(D_FF // FF_CHUNK):
            up = jnp.dot(hb, wup_ref[:, f * FF_CHUNK:(f + 1) * FF_CHUNK], preferred_element_type=F32)
            act = jnp.square(jnp.maximum(up, 0.0)).astype(BF16)
            ff = ff + jnp.dot(act, wdn_ref[f * FF_CHUNK:(f + 1) * FF_CHUNK, :], preferred_element_type=F32)
        return alpha * h + ff

    ln1 = lambda v: _layer_norm_rows(v, g1_ref[...], b1_ref[...])
    ln2 = lambda v: _layer_norm_rows(v, g2_ref[...], b2_ref[...])
    hs = [ln1(mix_pre_ln(rows)) for rows in parts]
    for rows, h in zip(parts, hs):
        out_ref[rows, :] = ln2(mlp_pre_ln(h))


def _run_tail(x2, ya2, yb2, wg, wbr, wo, wup, wdn, g1, b1, g2, b2, alpha):
    n, d = x2.shape
    const = lambda *shape: pl.BlockSpec(shape, lambda r: (0,) * len(shape),
                                        pipeline_mode=pl.Buffered(1))
    rows = lambda width: pl.BlockSpec((TM_TAIL, width), lambda r: (r, 0))
    return pl.pallas_call(
        functools.partial(_tail_kernel, alpha=alpha),
        grid=(n // TM_TAIL,),
        in_specs=[rows(d), rows(D_CONV), rows(D_ATTN),
                  const(N_BRANCH * d, d), const(N_BRANCH, D_CONV, d), const(d, d),
                  const(d, D_FF), const(D_FF, d),
                  const(1, d), const(1, d), const(1, d), const(1, d)],
        out_specs=rows(d),
        out_shape=jax.ShapeDtypeStruct((n, d), F32),
        compiler_params=pltpu.CompilerParams(
            dimension_semantics=("parallel",), vmem_limit_bytes=VMEM_LIMIT),
        name="merge_mlp",
    )(x2, ya2, yb2, wg, wbr, wo, wup, wdn, g1, b1, g2, b2)


def kernel(x, w_in, conv_w, idx_k_norm_g, idx_k_norm_b, w_branch, w_o, ln1_g, ln1_b, w_up, w_down, ln2_g, ln2_b):
    b, s, d = x.shape
    depth = w_in.shape[0]
    assert d == D_MODEL and s % TM_PROJ == 0 and (b * s) % TM_TAIL == 0 and TQ == TK
    k_sel = min(TOPK_MAX, s // 4)
    alpha = (2 * depth) ** 0.25
    for l in range(depth):
        wt = jnp.swapaxes(w_in[l], 0, 1)
        pad = lambda a: jnp.pad(a.astype(F32), (0, LANES - IDX_DIM)).reshape(1, LANES)
        later_weights = (w_branch[l].reshape(N_BRANCH * D_CONV, d), w_o[l], w_up[l], w_down[l])
        ya, q, k, vt, qi, ki, wi, wg, wbr, wo, wup, wdn = _run_proj(
            x, wt, conv_w[l], pad(idx_k_norm_g[l]), pad(idx_k_norm_b[l]), later_weights)
        yb = _run_attn(qi, wi, ki, q, k, vt, k_sel)
        row = lambda a: a.astype(F32).reshape(1, d)
        out = _run_tail(x.reshape(b * s, d), ya.reshape(b * s, D_CONV), yb.reshape(b * s, D_ATTN),
                        wg, wbr.reshape(N_BRANCH, D_CONV, d), wo, wup, wdn,
                        row(ln1_g[l]), row(ln1_b[l]), row(ln2_g[l]), row(ln2_b[l]), alpha)
        x = out.reshape(b, s, d)
    return x
```

```python
import functools

import jax
import jax.numpy as jnp
from jax import lax
from jax.experimental import pallas as pl
from jax.experimental.pallas import tpu as pltpu

D_MODEL = 1024
D_CONV = 512
CONV_WIDTH = 3
N_HEADS = 8
N_KV_HEADS = 2
HEAD_DIM = 64
D_ATTN = N_HEADS * HEAD_DIM
D_KV = N_KV_HEADS * HEAD_DIM
IDX_HEADS = 8
IDX_DIM = 64
TOPK_MAX = 256
N_BRANCH = 2
D_FF = 4 * D_MODEL
LN_EPS = 1e-5
IDX_SCALE = (IDX_DIM ** -0.5) * (IDX_HEADS ** -0.5)
ATTN_SCALE = HEAD_DIM ** -0.5
LOG2_E = 1.4426950408889634
Q_SCALE = ATTN_SCALE * LOG2_E

OFF_Q = 3 * D_CONV
OFF_K = OFF_Q + D_ATTN
OFF_V = OFF_K + D_KV
OFF_QI = OFF_V + D_KV
OFF_KI = OFF_QI + IDX_HEADS * IDX_DIM
OFF_WI = OFF_KI + IDX_DIM
OFF_GATES = OFF_WI + IDX_HEADS

LANES = 128
SUBLANES = 8
BF16_SUBLANES = 16
VMEM_LIMIT = 56 * 1024 * 1024

TM_PROJ = 1024
TQ = 256
TK = 256
TM_TAIL = 512
TAIL_PARTS = 2
FF_CHUNK = 1024
COUNT_CHAINS = 4
EXP_ROWS = 32

F32 = jnp.float32
BF16 = jnp.bfloat16
NEG_INF = float("-inf")
F32_MAX = float(jnp.finfo(jnp.float32).max)
INT_MIN = -2 ** 31
KEY_BITS = 32
CAST_ROWS = (N_BRANCH * D_CONV, D_MODEL, D_MODEL, D_FF)


def _dot_nt(a, b_t):
    return lax.dot_general(a, b_t, (((1,), (1,)), ((), ())), preferred_element_type=F32)


def _layer_norm_rows(x, g, b):
    mu = jnp.mean(x, axis=-1, keepdims=True)
    xc = x - mu
    var = jnp.mean(xc * xc, axis=-1, keepdims=True)
    return xc * lax.rsqrt(var + LN_EPS) * g + b


def _proj_kernel(x_ref, wt_ref, convw_ref, kg_ref, kb_ref, *rest):
    n_cast = len(CAST_ROWS)
    cast_in, rest = rest[:n_cast], rest[n_cast:]
    ya_ref, q_ref, k_ref, vt_ref, qi_ref, ki_ref, wi_ref, wg_ref = rest[:8]
    cast_out, carry_ref = rest[8:8 + n_cast], rest[8 + n_cast]
    for src, dst in zip(cast_in, cast_out):
        dst[...] = src[...].astype(BF16)
    step = pl.program_id(0) * pl.num_programs(1) + pl.program_id(1)
    gate_rows = wg_ref.shape[0]
    wg_ref[...] = wt_ref[pl.ds(pl.multiple_of(OFF_GATES + step * gate_rows, SUBLANES), gate_rows), :].astype(BF16)

    @pl.when(pl.program_id(1) == 0)
    def _():
        carry_ref[...] = jnp.zeros_like(carry_ref)

    xb = x_ref[0].astype(BF16)

    def proj(lo, hi):
        return _dot_nt(xb, wt_ref[lo:hi, :].astype(BF16))

    zk = _dot_nt(xb, wt_ref[OFF_KI:OFF_KI + LANES, :].astype(BF16))
    wi_ref[0] = zk
    valid = lax.broadcasted_iota(jnp.int32, zk.shape, 1) < IDX_DIM
    mu = jnp.sum(jnp.where(valid, zk, 0.0), axis=-1, keepdims=True) * (1.0 / IDX_DIM)
    xc = jnp.where(valid, zk - mu, 0.0)
    var = jnp.sum(xc * xc, axis=-1, keepdims=True) * (1.0 / IDX_DIM)
    kin = xc * lax.rsqrt(var + LN_EPS) * kg_ref[...] + kb_ref[...]
    ki_ref[0] = kin[:, :IDX_DIM].astype(BF16)

    vt = proj(OFF_V, OFF_QI).T
    for c in range(TM_PROJ // TK):
        vt_ref[0, c] = vt[:, c * TK:(c + 1) * TK].astype(BF16)

    cu = proj(D_CONV, 2 * D_CONV) * proj(2 * D_CONV, 3 * D_CONV)
    prev = carry_ref[...]
    row = lax.broadcasted_iota(jnp.int32, cu.shape, 0)
    cu1 = jnp.where(row == 0, prev[SUBLANES - 1:SUBLANES, :], pltpu.roll(cu, 1, axis=0))
    cu2 = jnp.where(row == 0, prev[SUBLANES - 2:SUBLANES - 1, :],
                    jnp.where(row == 1, prev[SUBLANES - 1:SUBLANES, :], pltpu.roll(cu, 2, axis=0)))
    carry_ref[...] = cu[TM_PROJ - SUBLANES:, :]
    cw = convw_ref[...]
    conv = cw[0:1, :] * cu2 + cw[1:2, :] * cu1 + cw[2:3, :] * cu
    ya_ref[0] = (proj(0, D_CONV) * conv).astype(BF16)

    k_ref[0] = proj(OFF_K, OFF_V).astype(BF16)
    q_ref[0] = (proj(OFF_Q, OFF_K) * Q_SCALE).astype(BF16)
    qi_ref[0] = proj(OFF_QI, OFF_KI).astype(BF16)


def _run_proj(x, wt, conv_w, kg, kb, cast_weights):
    b, s, d = x.shape
    nc = s // TK
    cpt = TM_PROJ // TK
    steps = s // TM_PROJ
    n_steps = b * steps
    assert tuple(w.shape[0] for w in cast_weights) == CAST_ROWS
    gate_rows = wt.shape[0] - OFF_GATES
    cast_specs = [pl.BlockSpec((w.shape[0] // n_steps, w.shape[1]), lambda bi, j: (bi * steps + j, 0))
                  for w in cast_weights]
    const = lambda *shape: pl.BlockSpec(shape, lambda bi, j: (0,) * len(shape),
                                        pipeline_mode=pl.Buffered(1))
    rows = lambda width: pl.BlockSpec((1, TM_PROJ, width), lambda bi, j: (bi, j, 0))
    return pl.pallas_call(
        _proj_kernel,
        grid=(b, s // TM_PROJ),
        in_specs=[rows(d), const(*wt.shape), const(CONV_WIDTH, D_CONV),
                  const(1, LANES), const(1, LANES)] + cast_specs,
        out_specs=[rows(D_CONV), rows(D_ATTN), rows(D_KV),
                   pl.BlockSpec((1, cpt, D_KV, TK), lambda bi, j: (bi, j, 0, 0)),
                   rows(IDX_HEADS * IDX_DIM), rows(IDX_DIM), rows(LANES),
                   pl.BlockSpec((gate_rows // n_steps, d), lambda bi, j: (bi * steps + j, 0))] + cast_specs,
        out_shape=[jax.ShapeDtypeStruct((b, s, D_CONV), BF16),
                   jax.ShapeDtypeStruct((b, s, D_ATTN), BF16),
                   jax.ShapeDtypeStruct((b, s, D_KV), BF16),
                   jax.ShapeDtypeStruct((b, nc, D_KV, TK), BF16),
                   jax.ShapeDtypeStruct((b, s, IDX_HEADS * IDX_DIM), BF16),
                   jax.ShapeDtypeStruct((b, s, IDX_DIM), BF16),
                   jax.ShapeDtypeStruct((b, s, LANES), F32),
                   jax.ShapeDtypeStruct((gate_rows, d), BF16)]
        + [jax.ShapeDtypeStruct(w.shape, BF16) for w in cast_weights],
        scratch_shapes=[pltpu.VMEM((SUBLANES, D_CONV), F32)],
        compiler_params=pltpu.CompilerParams(
            dimension_semantics=("parallel", "arbitrary"), vmem_limit_bytes=VMEM_LIMIT),
        name="proj_mixer_a",
    )(x, wt, conv_w, kg, kb, *cast_weights)


def _sortable_to_f32(x):
    bits = x ^ ((x >> 31) & jnp.int32(0x7FFFFFFF))
    return pltpu.bitcast(bits, F32)


def _attn_kernel(qi_ref, wi_ref, ki_ref, q_ref, k_ref, vt_ref, yb_ref,
                 score_ref, bias_ref, qit_ref, qpad_ref, s0_ref, s1_ref, p0_ref, p1_ref,
                 mc0_ref, mc1_ref, alpha_ref, m_ref, l_ref, acc_ref, thr_ref, tau_ref, cntp_ref, *, k_sel):
    i = pl.program_id(1)
    nk = i + 1
    kf = float(k_sel)
    rep_heads = N_HEADS // N_KV_HEADS
    n_groups = TK // SUBLANES
    n_lane_groups = TQ // LANES

    qit = qi_ref[0].astype(F32).T.astype(BF16)
    for h in range(IDX_HEADS):
        qit_ref[:, h * TQ:(h + 1) * TQ] = qit[h * IDX_DIM:(h + 1) * IDX_DIM, :]
    wt = wi_ref[0].T[IDX_DIM:IDX_DIM + IDX_HEADS, :]
    qt = q_ref[0].astype(F32).T.astype(BF16)
    zero_half = jnp.zeros((HEAD_DIM, TQ), BF16)
    for h in range(N_HEADS):
        g = h // rep_heads
        for gg in range(N_KV_HEADS):
            qpad_ref[gg * HEAD_DIM:(gg + 1) * HEAD_DIM, h * TQ:(h + 1) * TQ] = (
                qt[h * HEAD_DIM:(h + 1) * HEAD_DIM, :] if gg == g else zero_half)

    row = lax.broadcasted_iota(jnp.int32, (TK, TQ), 0)
    col = lax.broadcasted_iota(jnp.int32, (TK, TQ), 1)
    row_minus_col = row - col

    def score_chunk(c):
        kic = ki_ref[0, c]
        lg_all = jnp.dot(kic, qit_ref[...], preferred_element_type=F32)
        acc = jnp.zeros((TK, TQ), F32)
        for h in range(IDX_HEADS):
            lg = lg_all[:, h * TQ:(h + 1) * TQ]
            acc = acc + wt[h:h + 1, :] * jnp.maximum(lg, 0.0)
        noncausal = row_minus_col > (i - c) * TQ
        score_ref[c] = jnp.where(noncausal, NEG_INF, acc * IDX_SCALE)

    def score_quad(j, carry):
        for t in range(4):
            score_chunk(4 * j + t)
        return carry

    lax.fori_loop(0, nk // 4, score_quad, 0)

    @pl.when(nk % 4 >= 2)
    def _():
        score_chunk(nk - nk % 4)
        score_chunk(nk - nk % 4 + 1)

    @pl.when(nk % 2 == 1)
    def _():
        score_chunk(nk - 1)

    def count_where(pred, thr):
        thr_ref[...] = jnp.broadcast_to(thr, (SUBLANES, TQ))
        thr_g = [thr_ref[:, g * LANES:(g + 1) * LANES] for g in range(n_lane_groups)]

        def chunk_counts(c, accs, diagonal):
            accs = [list(a) for a in accs]
            for r in range(n_groups):
                for g in range(n_lane_groups):
                    if diagonal and r * SUBLANES >= (g + 1) * LANES:
                        continue
                    blk = score_ref[c, r * SUBLANES:(r + 1) * SUBLANES, g * LANES:(g + 1) * LANES]
                    a = accs[g][r % COUNT_CHAINS]
                    accs[g][r % COUNT_CHAINS] = jnp.where(pred(blk, thr_g[g]), a + 1.0, a)
            return tuple(tuple(a) for a in accs)

        zero = jnp.zeros((SUBLANES, LANES), F32)
        accs = ((zero,) * COUNT_CHAINS,) * n_lane_groups
        n_full = nk - 1
        accs = lax.fori_loop(
            0, n_full // 2,
            lambda j, a: chunk_counts(2 * j + 1, chunk_counts(2 * j, a, False), False), accs)
        accs = lax.cond(n_full % 2 == 1, lambda a: chunk_counts(n_full - 1, a, False), lambda a: a, accs)
        accs = chunk_counts(nk - 1, accs, True)
        per_group = [sum(a[1:], a[0]) for a in accs]
        return jnp.sum(jnp.concatenate(per_group, axis=1), axis=0, keepdims=True)

    def count_ge(cand_key):
        return count_where(lambda blk, thr: blk >= thr, _sortable_to_f32(cand_key))

    def bit_body(b, carry):
        prefix, cntp = carry
        cand_key = prefix | jnp.left_shift(jnp.int32(1), KEY_BITS - 2 - b)
        cnt = count_ge(cand_key)
        ok = cnt >= kf
        return jnp.where(ok, cand_key, prefix), jnp.where(ok, cnt, cntp)

    tau_ref[...] = jnp.full(tau_ref.shape, -F32_MAX, F32)
    cntp_ref[...] = jnp.zeros(cntp_ref.shape, F32)

    @pl.when(nk * TK > k_sel)
    def _():
        cnt0 = count_ge(jnp.zeros((1, TQ), jnp.int32))
        ok0 = cnt0 >= kf
        prefix0 = jnp.where(ok0, jnp.int32(0), jnp.int32(INT_MIN))
        prefix, cntp = lax.fori_loop(0, KEY_BITS - 1, bit_body, (prefix0, jnp.where(ok0, cnt0, 0.0)))
        tau_found = jnp.where(prefix == INT_MIN, -F32_MAX, _sortable_to_f32(prefix))
        tau_ref[...] = jnp.broadcast_to(tau_found, tau_ref.shape)
        cntp_ref[...] = jnp.broadcast_to(cntp, cntp_ref.shape)

    tau = tau_ref[0:1, :]
    cntp = cntp_ref[0:1, :]

    def bias_body(c, carry):
        bias_ref[c] = jnp.where(score_ref[c] >= tau, 0.0, NEG_INF)
        return carry

    lax.fori_loop(0, nk, bias_body, 0)

    @pl.when(jnp.max(cntp) > kf)
    def _():
        need = kf - count_where(lambda blk, thr: blk > thr, tau)
        r = lax.broadcasted_iota(jnp.int32, (TK, TK), 0)
        cc = lax.broadcasted_iota(jnp.int32, (TK, TK), 1)
        before = jnp.where(cc < r, 1.0, 0.0).astype(BF16)

        def tie_body(c, seen):
            sc = score_ref[c]
            eq = sc == tau
            eqf = jnp.where(eq, 1.0, 0.0)
            rank = jnp.dot(before, eqf.astype(BF16), preferred_element_type=F32) + seen
            keep = (sc > tau) | (eq & (rank < need))
            bias_ref[c] = jnp.where(keep, 0.0, NEG_INF)
            return seen + jnp.sum(eqf, axis=0, keepdims=True)

        lax.fori_loop(0, nk, tie_body, jnp.zeros((1, TQ), F32))

    m_ref[...] = jnp.full(m_ref.shape, NEG_INF, F32)
    l_ref[...] = jnp.zeros(l_ref.shape, F32)
    acc_ref[...] = jnp.zeros(acc_ref.shape, F32)
    alpha_ref[...] = jnp.ones(alpha_ref.shape, F32)
    p1_ref[...] = jnp.zeros(p1_ref.shape, BF16)
    ones_rows = jnp.ones((BF16_SUBLANES, TK), BF16)

    def logits_head(h, kc, bias, s_out, mc_out):
        cols = slice(h * TQ, (h + 1) * TQ)
        s = jnp.dot(kc, qpad_ref[:, cols], preferred_element_type=F32) + bias
        s_out[:, cols] = s
        mc_out[h] = jnp.broadcast_to(jnp.max(s, axis=0, keepdims=True), (SUBLANES, TQ))

    def pv_update(c, p_in):
        for g in range(N_KV_HEADS):
            lanes = slice(g * rep_heads * TQ, (g + 1) * rep_heads * TQ)
            lhs = jnp.concatenate([vt_ref[0, c, g * HEAD_DIM:(g + 1) * HEAD_DIM, :], ones_rows], axis=0)
            pv = jnp.dot(lhs, p_in[:, lanes], preferred_element_type=F32)
            for hh in range(rep_heads):
                h = g * rep_heads + hh
                rows = slice(h * HEAD_DIM, (h + 1) * HEAD_DIM)
                cols = slice(hh * TQ, (hh + 1) * TQ)
                alpha = alpha_ref[h, 0:1, :]
                acc_ref[rows, :] = alpha * acc_ref[rows, :] + pv[:HEAD_DIM, cols]
                l_ref[h] = jnp.broadcast_to(alpha * l_ref[h, 0:1, :] + pv[HEAD_DIM:HEAD_DIM + 1, cols],
                                            (SUBLANES, TQ))

    def softmax_head(h, s_in, mc_in, p_out):
        m_old = m_ref[h, 0:1, :]
        m_new = jnp.maximum(m_old, mc_in[h, 0:1, :])
        m_use = jnp.where(m_new == NEG_INF, 0.0, m_new)
        alpha_ref[h] = jnp.broadcast_to(jnp.exp2(m_old - m_use), (SUBLANES, TQ))
        m_ref[h] = jnp.broadcast_to(m_new, (SUBLANES, TQ))
        cols = slice(h * TQ, (h + 1) * TQ)
        for blk in range(TK // EXP_ROWS):
            rows = slice(blk * EXP_ROWS, (blk + 1) * EXP_ROWS)
            p_out[rows, cols] = jnp.exp2(s_in[rows, cols] - m_use).astype(BF16)

    def stage(c, s_cur, mc_cur, p_cur, s_nxt, mc_nxt, p_prev):
        c_nxt = jnp.minimum(c + 1, nk - 1)
        pv_update(jnp.maximum(c - 1, 0), p_prev)
        bias_nxt = bias_ref[c_nxt]
        k_nxt = k_ref[0, c_nxt]
        for h in range(N_HEADS):
            logits_head(h, k_nxt, bias_nxt, s_nxt, mc_nxt)
            softmax_head(h, s_cur, mc_cur, p_cur)

    for h in range(N_HEADS):
        logits_head(h, k_ref[0, 0], bias_ref[0], s0_ref, mc0_ref)

    def pair_body(j, carry):
        c = 2 * j
        stage(c, s0_ref, mc0_ref, p0_ref, s1_ref, mc1_ref, p1_ref)

        @pl.when(c + 1 < nk)
        def _():
            stage(c + 1, s1_ref, mc1_ref, p1_ref, s0_ref, mc0_ref, p0_ref)

        return carry

    lax.fori_loop(0, (nk + 1) // 2, pair_body, 0)

    @pl.when((nk - 1) % 2 == 0)
    def _():
        pv_update(nk - 1, p0_ref)

    @pl.when((nk - 1) % 2 == 1)
    def _():
        pv_update(nk - 1, p1_ref)

    for h in range(N_HEADS):
        rows = slice(h * HEAD_DIM, (h + 1) * HEAD_DIM)
        acc_ref[rows, :] = acc_ref[rows, :] / l_ref[h, 0:1, :]
    yb_ref[0] = acc_ref[...].T.astype(BF16)


def _run_attn(qi, wi, ki, q, k, vt, k_sel):
    b, s, _ = q.shape
    nc = s // TK
    ki = ki.reshape(b, nc, TK, IDX_DIM)
    k = k.reshape(b, nc, TK, D_KV)
    rows = lambda width: pl.BlockSpec((1, TQ, width), lambda bi, i: (bi, i, 0))
    keys = lambda d0, d1: pl.BlockSpec((1, nc, d0, d1), lambda bi, i: (bi, 0, 0, 0))
    return pl.pallas_call(
        functools.partial(_attn_kernel, k_sel=k_sel),
        grid=(b, s // TQ),
        in_specs=[rows(IDX_HEADS * IDX_DIM), rows(LANES), keys(TK, IDX_DIM),
                  rows(D_ATTN), keys(TK, D_KV), keys(D_KV, TK)],
        out_specs=rows(D_ATTN),
        out_shape=jax.ShapeDtypeStruct((b, s, D_ATTN), BF16),
        scratch_shapes=[pltpu.VMEM((nc, TK, TQ), F32),
                        pltpu.VMEM((nc, TK, TQ), F32),
                        pltpu.VMEM((IDX_DIM, IDX_HEADS * TQ), BF16),
                        pltpu.VMEM((D_KV, N_HEADS * TQ), BF16),
                        pltpu.VMEM((TK, N_HEADS * TQ), F32),
                        pltpu.VMEM((TK, N_HEADS * TQ), F32),
                        pltpu.VMEM((TK, N_HEADS * TQ), BF16),
                        pltpu.VMEM((TK, N_HEADS * TQ), BF16),
                        pltpu.VMEM((N_HEADS, SUBLANES, TQ), F32),
                        pltpu.VMEM((N_HEADS, SUBLANES, TQ), F32),
                        pltpu.VMEM((N_HEADS, SUBLANES, TQ), F32),
                        pltpu.VMEM((N_HEADS, SUBLANES, TQ), F32),
                        pltpu.VMEM((N_HEADS, SUBLANES, TQ), F32),
                        pltpu.VMEM((D_ATTN, TQ), F32),
                        pltpu.VMEM((SUBLANES, TQ), F32),
                        pltpu.VMEM((SUBLANES, TQ), F32),
                        pltpu.VMEM((SUBLANES, TQ), F32)],
        compiler_params=pltpu.CompilerParams(
            dimension_semantics=("parallel", "parallel"), vmem_limit_bytes=VMEM_LIMIT),
        name="dsa_attention",
    )(qi, wi, ki, q, k, vt)


def _tail_kernel(x_ref, ya_ref, yb_ref, wg_ref, wbr_ref, wo_ref, wup_ref, wdn_ref,
                 g1_ref, b1_ref, g2_ref, b2_ref, out_ref, *, alpha):
    part = TM_TAIL // TAIL_PARTS
    parts = [slice(p * part, (p + 1) * part) for p in range(TAIL_PARTS)]

    def mix_pre_ln(rows):
        x = x_ref[rows, :]
        xb = x.astype(BF16)
        merged = jnp.zeros((part, D_MODEL), F32)
        for n, y_ref in enumerate((ya_ref, yb_ref)):
            gate = _dot_nt(xb, wg_ref[n * D_MODEL:(n + 1) * D_MODEL, :])
            branch = jnp.dot(y_ref[rows, :], wbr_ref[n], preferred_element_type=F32)
            merged = merged + jax.nn.sigmoid(gate) * branch
        return alpha * x + jnp.dot(merged.astype(BF16), wo_ref[...], preferred_element_type=F32)

    def mlp_pre_ln(h):
        hb = h.astype(BF16)
        ff = jnp.zeros((part, D_MODEL), F32)
        for f in range(D_FF // FF_CHUNK):
            up = jnp.dot(hb, wup_ref[:, f * FF_CHUNK:(f + 1) * FF_CHUNK], preferred_element_type=F32)
            act = jnp.square(jnp.maximum(up, 0.0)).astype(BF16)
            ff = ff + jnp.dot(act, wdn_ref[f * FF_CHUNK:(f + 1) * FF_CHUNK, :], preferred_element_type=F32)
        return alpha * h + ff

    ln1 = lambda v: _layer_norm_rows(v, g1_ref[...], b1_ref[...])
    ln2 = lambda v: _layer_norm_rows(v, g2_ref[...], b2_ref[...])
    hs = [ln1(mix_pre_ln(rows)) for rows in parts]
    for rows, h in zip(parts, hs):
        out_ref[rows, :] = ln2(mlp_pre_ln(h))


def _run_tail(x2, ya2, yb2, wg, wbr, wo, wup, wdn, g1, b1, g2, b2, alpha):
    n, d = x2.shape
    const = lambda *shape: pl.BlockSpec(shape, lambda r: (0,) * len(shape),
                                        pipeline_mode=pl.Buffered(1))
    rows = lambda width: pl.BlockSpec((TM_TAIL, width), lambda r: (r, 0))
    return pl.pallas_call(
        functools.partial(_tail_kernel, alpha=alpha),
        grid=(n // TM_TAIL,),
        in_specs=[rows(d), rows(D_CONV), rows(D_ATTN),
                  const(N_BRANCH * d, d), const(N_BRANCH, D_CONV, d), const(d, d),
                  const(d, D_FF), const(D_FF, d),
                  const(1, d), const(1, d), const(1, d), const(1, d)],
        out_specs=rows(d),
        out_shape=jax.ShapeDtypeStruct((n, d), F32),
        compiler_params=pltpu.CompilerParams(
            dimension_semantics=("parallel",), vmem_limit_bytes=VMEM_LIMIT),
        name="merge_mlp",
    )(x2, ya2, yb2, wg, wbr, wo, wup, wdn, g1, b1, g2, b2)


def kernel(x, w_in, conv_w, idx_k_norm_g, idx_k_norm_b, w_branch, w_o, ln1_g, ln1_b, w_up, w_down, ln2_g, ln2_b):
    b, s, d = x.shape
    depth = w_in.shape[0]
    assert d == D_MODEL and s % TM_PROJ == 0 and (b * s) % TM_TAIL == 0 and TQ == TK
    k_sel = min(TOPK_MAX, s // 4)
    alpha = (2 * depth) ** 0.25
    for l in range(depth):
        wt = jnp.swapaxes(w_in[l], 0, 1)
        pad = lambda a: jnp.pad(a.astype(F32), (0, LANES - IDX_DIM)).reshape(1, LANES)
        later_weights = (w_branch[l].reshape(N_BRANCH * D_CONV, d), w_o[l], w_up[l], w_down[l])
        ya, q, k, vt, qi, ki, wi, wg, wbr, wo, wup, wdn = _run_proj(
            x, wt, conv_w[l], pad(idx_k_norm_g[l]), pad(idx_k_norm_b[l]), later_weights)
        yb = _run_attn(qi, wi, ki, q, k, vt, k_sel)
        row = lambda a: a.astype(F32).reshape(1, d)
        out = _run_tail(x.reshape(b * s, d), ya.reshape(b * s, D_CONV), yb.reshape(b * s, D_ATTN),
                        wg, wbr.reshape(N_BRANCH, D_CONV, d), wo, wup, wdn,
                        row(ln1_g[l]), row(ln1_b[l]), row(ln2_g[l]), row(ln2_b[l]), alpha)
        x = out.reshape(b, s, d)
    return x
```

```python
import functools

import jax
import jax.numpy as jnp
from jax import lax
from jax.experimental import pallas as pl
from jax.experimental.pallas import tpu as pltpu

D_MODEL = 1024
D_CONV = 512
CONV_WIDTH = 3
N_HEADS = 8
N_KV_HEADS = 2
HEAD_DIM = 64
D_ATTN = N_HEADS * HEAD_DIM
D_KV = N_KV_HEADS * HEAD_DIM
IDX_HEADS = 8
IDX_DIM = 64
TOPK_MAX = 256
N_BRANCH = 2
D_FF = 4 * D_MODEL
LN_EPS = 1e-5
IDX_SCALE = (IDX_DIM ** -0.5) * (IDX_HEADS ** -0.5)
ATTN_SCALE = HEAD_DIM ** -0.5
LOG2_E = 1.4426950408889634
Q_SCALE = ATTN_SCALE * LOG2_E

OFF_Q = 3 * D_CONV
OFF_K = OFF_Q + D_ATTN
OFF_V = OFF_K + D_KV
OFF_QI = OFF_V + D_KV
OFF_KI = OFF_QI + IDX_HEADS * IDX_DIM
OFF_WI = OFF_KI + IDX_DIM
OFF_GATES = OFF_WI + IDX_HEADS

LANES = 128
SUBLANES = 8
BF16_SUBLANES = 16
VMEM_LIMIT = 56 * 1024 * 1024

TM_PROJ = 1024
TQ = 256
TK = 256
TM_TAIL = 512
TAIL_PARTS = 2
FF_CHUNK = 1024
COUNT_CHAINS = 4
EXP_ROWS = 32

F32 = jnp.float32
BF16 = jnp.bfloat16
NEG_INF = float("-inf")
F32_MAX = float(jnp.finfo(jnp.float32).max)
INT_MIN = -2 ** 31
KEY_BITS = 32
CAST_ROWS = (N_BRANCH * D_CONV, D_MODEL, D_MODEL, D_FF)


def _dot_nt(a, b_t):
    return lax.dot_general(a, b_t, (((1,), (1,)), ((), ())), preferred_element_type=F32)


def _layer_norm_rows(x, g, b):
    mu = jnp.mean(x, axis=-1, keepdims=True)
    xc = x - mu
    var = jnp.mean(xc * xc, axis=-1, keepdims=True)
    return xc * lax.rsqrt(var + LN_EPS) * g + b


def _proj_kernel(x_ref, wt_ref, convw_ref, kg_ref, kb_ref, *rest):
    n_cast = len(CAST_ROWS)
    cast_in, rest = rest[:n_cast], rest[n_cast:]
    ya_ref, q_ref, k_ref, vt_ref, qi_ref, ki_ref, wi_ref, wg_ref = rest[:8]
    cast_out, carry_ref = rest[8:8 + n_cast], rest[8 + n_cast]
    for src, dst in zip(cast_in, cast_out):
        dst[...] = src[...].astype(BF16)
    step = pl.program_id(0) * pl.num_programs(1) + pl.program_id(1)
    gate_rows = wg_ref.shape[0]
    wg_ref[...] = wt_ref[pl.ds(pl.multiple_of(OFF_GATES + step * gate_rows, SUBLANES), gate_rows), :].astype(BF16)

    @pl.when(pl.program_id(1) == 0)
    def _():
        carry_ref[...] = jnp.zeros_like(carry_ref)

    xb = x_ref[0].astype(BF16)

    def proj(lo, hi):
        return _dot_nt(xb, wt_ref[lo:hi, :].astype(BF16))

    zk = _dot_nt(xb, wt_ref[OFF_KI:OFF_KI + LANES, :].astype(BF16))
    wi_ref[0] = zk
    valid = lax.broadcasted_iota(jnp.int32, zk.shape, 1) < IDX_DIM
    mu = jnp.sum(jnp.where(valid, zk, 0.0), axis=-1, keepdims=True) * (1.0 / IDX_DIM)
    xc = jnp.where(valid, zk - mu, 0.0)
    var = jnp.sum(xc * xc, axis=-1, keepdims=True) * (1.0 / IDX_DIM)
    kin = xc * lax.rsqrt(var + LN_EPS) * kg_ref[...] + kb_ref[...]
    ki_ref[0] = kin[:, :IDX_DIM].astype(BF16)

    vt = proj(OFF_V, OFF_QI).T
    for c in range(TM_PROJ // TK):
        vt_ref[0, c] = vt[:, c * TK:(c + 1) * TK].astype(BF16)

    cu = proj(D_CONV, 2 * D_CONV) * proj(2 * D_CONV, 3 * D_CONV)
    prev = carry_ref[...]
    row = lax.broadcasted_iota(jnp.int32, cu.shape, 0)
    cu1 = jnp.where(row == 0, prev[SUBLANES - 1:SUBLANES, :], pltpu.roll(cu, 1, axis=0))
    cu2 = jnp.where(row == 0, prev[SUBLANES - 2:SUBLANES - 1, :],
                    jnp.where(row == 1, prev[SUBLANES - 1:SUBLANES, :], pltpu.roll(cu, 2, axis=0)))
    carry_ref[...] = cu[TM_PROJ - SUBLANES:, :]
    cw = convw_ref[...]
    conv = cw[0:1, :] * cu2 + cw[1:2, :] * cu1 + cw[2:3, :] * cu
    ya_ref[0] = (proj(0, D_CONV) * conv).astype(BF16)

    k_ref[0] = proj(OFF_K, OFF_V).astype(BF16)
    q_ref[0] = (proj(OFF_Q, OFF_K) * Q_SCALE).astype(BF16)
    qi_ref[0] = proj(OFF_QI, OFF_KI).astype(BF16)


def _run_proj(x, wt, conv_w, kg, kb, cast_weights):
    b, s, d = x.shape
    nc = s // TK
    cpt = TM_PROJ // TK
    steps = s // TM_PROJ
    n_steps = b * steps
    assert tuple(w.shape[0] for w in cast_weights) == CAST_ROWS
    gate_rows = wt.shape[0] - OFF_GATES
    cast_specs = [pl.BlockSpec((w.shape[0] // n_steps, w.shape[1]), lambda bi, j: (bi * steps + j, 0))
                  for w in cast_weights]
    const = lambda *shape: pl.BlockSpec(shape, lambda bi, j: (0,) * len(shape),
                                        pipeline_mode=pl.Buffered(1))
    rows = lambda width: pl.BlockSpec((1, TM_PROJ, width), lambda bi, j: (bi, j, 0))
    return pl.pallas_call(
        _proj_kernel,
        grid=(b, s // TM_PROJ),
        in_specs=[rows(d), const(*wt.shape), const(CONV_WIDTH, D_CONV),
                  const(1, LANES), const(1, LANES)] + cast_specs,
        out_specs=[rows(D_CONV), rows(D_ATTN), rows(D_KV),
                   pl.BlockSpec((1, cpt, D_KV, TK), lambda bi, j: (bi, j, 0, 0)),
                   rows(IDX_HEADS * IDX_DIM), rows(IDX_DIM), rows(LANES),
                   pl.BlockSpec((gate_rows // n_steps, d), lambda bi, j: (bi * steps + j, 0))] + cast_specs,
        out_shape=[jax.ShapeDtypeStruct((b, s, D_CONV), BF16),
                   jax.ShapeDtypeStruct((b, s, D_ATTN), BF16),
                   jax.ShapeDtypeStruct((b, s, D_KV), BF16),
                   jax.ShapeDtypeStruct((b, nc, D_KV, TK), BF16),
                   jax.ShapeDtypeStruct((b, s, IDX_HEADS * IDX_DIM), BF16),
                   jax.ShapeDtypeStruct((b, s, IDX_DIM), BF16),
                   jax.ShapeDtypeStruct((b, s, LANES), F32),
                   jax.ShapeDtypeStruct((gate_rows, d), BF16)]
        + [jax.ShapeDtypeStruct(w.shape, BF16) for w in cast_weights],
        scratch_shapes=[pltpu.VMEM((SUBLANES, D_CONV), F32)],
        compiler_params=pltpu.CompilerParams(
            dimension_semantics=("parallel", "arbitrary"), vmem_limit_bytes=VMEM_LIMIT),
        name="proj_mixer_a",
    )(x, wt, conv_w, kg, kb, *cast_weights)


def _sortable_to_f32(x):
    bits = x ^ ((x >> 31) & jnp.int32(0x7FFFFFFF))
    return pltpu.bitcast(bits, F32)


def _attn_kernel(qi_ref, wi_ref, ki_ref, q_ref, k_ref, vt_ref, yb_ref,
                 score_ref, bias_ref, qit_ref, qpad_ref, s0_ref, s1_ref, p0_ref, p1_ref,
                 mc0_ref, mc1_ref, alpha_ref, m_ref, l_ref, acc_ref, thr_ref, tau_ref, cntp_ref, *, k_sel):
    i = pl.program_id(1)
    nk = i + 1
    kf = float(k_sel)
    rep_heads = N_HEADS // N_KV_HEADS
    n_groups = TK // SUBLANES
    n_lane_groups = TQ // LANES

    qit = qi_ref[0].astype(F32).T.astype(BF16)
    for h in range(IDX_HEADS):
        qit_ref[:, h * TQ:(h + 1) * TQ] = qit[h * IDX_DIM:(h + 1) * IDX_DIM, :]
    wt = wi_ref[0].T[IDX_DIM:IDX_DIM + IDX_HEADS, :]
    qt = q_ref[0].astype(F32).T.astype(BF16)
    zero_half = jnp.zeros((HEAD_DIM, TQ), BF16)
    for h in range(N_HEADS):
        g = h // rep_heads
        for gg in range(N_KV_HEADS):
            qpad_ref[gg * HEAD_DIM:(gg + 1) * HEAD_DIM, h * TQ:(h + 1) * TQ] = (
                qt[h * HEAD_DIM:(h + 1) * HEAD_DIM, :] if gg == g else zero_half)

    row = lax.broadcasted_iota(jnp.int32, (TK, TQ), 0)
    col = lax.broadcasted_iota(jnp.int32, (TK, TQ), 1)
    row_minus_col = row - col

    def score_chunk(c):
        kic = ki_ref[0, c]
        lg_all = jnp.dot(kic, qit_ref[...], preferred_element_type=F32)
        acc = jnp.zeros((TK, TQ), F32)
        for h in range(IDX_HEADS):
            lg = lg_all[:, h * TQ:(h + 1) * TQ]
            acc = acc + wt[h:h + 1, :] * jnp.maximum(lg, 0.0)
        noncausal = row_minus_col > (i - c) * TQ
        score_ref[c] = jnp.where(noncausal, NEG_INF, acc * IDX_SCALE)

    def score_quad(j, carry):
        for t in range(4):
            score_chunk(4 * j + t)
        return carry

    lax.fori_loop(0, nk // 4, score_quad, 0)

    @pl.when(nk % 4 >= 2)
    def _():
        score_chunk(nk - nk % 4)
        score_chunk(nk - nk % 4 + 1)

    @pl.when(nk % 2 == 1)
    def _():
        score_chunk(nk - 1)

    def count_where(pred, thr):
        thr_ref[...] = jnp.broadcast_to(thr, (SUBLANES, TQ))
        thr_g = [thr_ref[:, g * LANES:(g + 1) * LANES] for g in range(n_lane_groups)]

        def chunk_counts(c, accs, diagonal):
            accs = [list(a) for a in accs]
            for r in range(n_groups):
                for g in range(n_lane_groups):
                    if diagonal and r * SUBLANES >= (g + 1) * LANES:
                        continue
                    blk = score_ref[c, r * SUBLANES:(r + 1) * SUBLANES, g * LANES:(g + 1) * LANES]
                    a = accs[g][r % COUNT_CHAINS]
                    accs[g][r % COUNT_CHAINS] = jnp.where(pred(blk, thr_g[g]), a + 1.0, a)
            return tuple(tuple(a) for a in accs)

        zero = jnp.zeros((SUBLANES, LANES), F32)
        accs = ((zero,) * COUNT_CHAINS,) * n_lane_groups
        n_full = nk - 1
        accs = lax.fori_loop(
            0, n_full // 2,
            lambda j, a: chunk_counts(2 * j + 1, chunk_counts(2 * j, a, False), False), accs)
        accs = lax.cond(n_full % 2 == 1, lambda a: chunk_counts(n_full - 1, a, False), lambda a: a, accs)
        accs = chunk_counts(nk - 1, accs, True)
        per_group = [sum(a[1:], a[0]) for a in accs]
        return jnp.sum(jnp.concatenate(per_group, axis=1), axis=0, keepdims=True)

    def count_ge(cand_key):
        return count_where(lambda blk, thr: blk >= thr, _sortable_to_f32(cand_key))

    def bit_body(b, carry):
        prefix, cntp = carry
        cand_key = prefix | jnp.left_shift(jnp.int32(1), KEY_BITS - 2 - b)
        cnt = count_ge(cand_key)
        ok = cnt >= kf
        return jnp.where(ok, cand_key, prefix), jnp.where(ok, cnt, cntp)

    tau_ref[...] = jnp.full(tau_ref.shape, -F32_MAX, F32)
    cntp_ref[...] = jnp.zeros(cntp_ref.shape, F32)

    @pl.when(nk * TK > k_sel)
    def _():
        cnt0 = count_ge(jnp.zeros((1, TQ), jnp.int32))
        ok0 = cnt0 >= kf
        prefix0 = jnp.where(ok0, jnp.int32(0), jnp.int32(INT_MIN))
        prefix, cntp = lax.fori_loop(0, KEY_BITS - 1, bit_body, (prefix0, jnp.where(ok0, cnt0, 0.0)))
        tau_found = jnp.where(prefix == INT_MIN, -F32_MAX, _sortable_to_f32(prefix))
        tau_ref[...] = jnp.broadcast_to(tau_found, tau_ref.shape)
        cntp_ref[...] = jnp.broadcast_to(cntp, cntp_ref.shape)

    tau = tau_ref[0:1, :]
    cntp = cntp_ref[0:1, :]

    def bias_body(c, carry):
        bias_ref[c] = jnp.where(score_ref[c] >= tau, 0.0, NEG_INF)
        return carry

    lax.fori_loop(0, nk, bias_body, 0)

    @pl.when(jnp.max(cntp) > kf)
    def _():
        need = kf - count_where(lambda blk, thr: blk > thr, tau)
        r = lax.broadcasted_iota(jnp.int32, (TK, TK), 0)
        cc = lax.broadcasted_iota(jnp.int32, (TK, TK), 1)
        before = jnp.where(cc < r, 1.0, 0.0).astype(BF16)

        def tie_body(c, seen):
            sc = score_ref[c]
            eq = sc == tau
            eqf = jnp.where(eq, 1.0, 0.0)
            rank = jnp.dot(before, eqf.astype(BF16), preferred_element_type=F32) + seen
            keep = (sc > tau) | (eq & (rank < need))
            bias_ref[c] = jnp.where(keep, 0.0, NEG_INF)
            return seen + jnp.sum(eqf, axis=0, keepdims=True)

        lax.fori_loop(0, nk, tie_body, jnp.zeros((1, TQ), F32))

    m_ref[...] = jnp.full(m_ref.shape, NEG_INF, F32)
    l_ref[...] = jnp.zeros(l_ref.shape, F32)
    acc_ref[...] = jnp.zeros(acc_ref.shape, F32)
    ones_rows = jnp.ones((BF16_SUBLANES, TK), BF16)

    def logits_head(h, kc, bias, s_out, mc_out):
        cols = slice(h * TQ, (h + 1) * TQ)
        s = jnp.dot(kc, qpad_ref[:, cols], preferred_element_type=F32) + bias
        s_out[:, cols] = s
        mc_out[h] = jnp.broadcast_to(jnp.max(s, axis=0, keepdims=True), (SUBLANES, TQ))

    def pv_update(c, p_in):
        for g in range(N_KV_HEADS):
            lanes = slice(g * rep_heads * TQ, (g + 1) * rep_heads * TQ)
            lhs = jnp.concatenate([vt_ref[0, c, g * HEAD_DIM:(g + 1) * HEAD_DIM, :], ones_rows], axis=0)
            pv = jnp.dot(lhs, p_in[:, lanes], preferred_element_type=F32)
            for hh in range(rep_heads):
                h = g * rep_heads + hh
                rows = slice(h * HEAD_DIM, (h + 1) * HEAD_DIM)
                cols = slice(hh * TQ, (hh + 1) * TQ)
                alpha = alpha_ref[h, 0:1, :]
                acc_ref[rows, :] = alpha * acc_ref[rows, :] + pv[:HEAD_DIM, cols]
                l_ref[h] = jnp.broadcast_to(alpha * l_ref[h, 0:1, :] + pv[HEAD_DIM:HEAD_DIM + 1, cols],
                                            (SUBLANES, TQ))

    def softmax_head(h, s_in, mc_in, p_out):
        m_old = m_ref[h, 0:1, :]
        m_new = jnp.maximum(m_old, mc_in[h, 0:1, :])
        m_use = jnp.where(m_new == NEG_INF, 0.0, m_new)
        alpha_ref[h] = jnp.broadcast_to(jnp.exp2(m_old - m_use), (SUBLANES, TQ))
        m_ref[h] = jnp.broadcast_to(m_new, (SUBLANES, TQ))
        cols = slice(h * TQ, (h + 1) * TQ)
        for blk in range(TK // EXP_ROWS):
            rows = slice(blk * EXP_ROWS, (blk + 1) * EXP_ROWS)
            p_out[rows, cols] = jnp.exp2(s_in[rows, cols] - m_use).astype(BF16)

    def stage(c, s_cur, mc_cur, p_cur, s_nxt, mc_nxt, p_prev):
        c_nxt = jnp.minimum(c + 1, nk - 1)
        if p_prev is not None:
            pv_update(c - 1, p_prev)
        bias_nxt = bias_ref[c_nxt]
        k_nxt = k_ref[0, c_nxt]
        for h in range(N_HEADS):
            logits_head(h, k_nxt, bias_nxt, s_nxt, mc_nxt)
            softmax_head(h, s_cur, mc_cur, p_cur)

    for h in range(N_HEADS):
        logits_head(h, k_ref[0, 0], bias_ref[0], s0_ref, mc0_ref)

    def pair_stages(c, first):
        stage(c, s0_ref, mc0_ref, p0_ref, s1_ref, mc1_ref, None if first else p1_ref)

        @pl.when(c + 1 < nk)
        def _():
            stage(c + 1, s1_ref, mc1_ref, p1_ref, s0_ref, mc0_ref, p0_ref)

    pair_stages(0, True)

    def pair_body(j, carry):
        pair_stages(2 * j, False)
        return carry

    lax.fori_loop(1, (nk + 1) // 2, pair_body, 0)

    @pl.when((nk - 1) % 2 == 0)
    def _():
        pv_update(nk - 1, p0_ref)

    @pl.when((nk - 1) % 2 == 1)
    def _():
        pv_update(nk - 1, p1_ref)

    for h in range(N_HEADS):
        rows = slice(h * HEAD_DIM, (h + 1) * HEAD_DIM)
        acc_ref[rows, :] = acc_ref[rows, :] / l_ref[h, 0:1, :]
    yb_ref[0] = acc_ref[...].T.astype(BF16)


def _run_attn(qi, wi, ki, q, k, vt, k_sel):
    b, s, _ = q.shape
    nc = s // TK
    ki = ki.reshape(b, nc, TK, IDX_DIM)
    k = k.reshape(b, nc, TK, D_KV)
    rows = lambda width: pl.BlockSpec((1, TQ, width), lambda bi, i: (bi, i, 0))
    keys = lambda d0, d1: pl.BlockSpec((1, nc, d0, d1), lambda bi, i: (bi, 0, 0, 0))
    return pl.pallas_call(
        functools.partial(_attn_kernel, k_sel=k_sel),
        grid=(b, s // TQ),
        in_specs=[rows(IDX_HEADS * IDX_DIM), rows(LANES), keys(TK, IDX_DIM),
                  rows(D_ATTN), keys(TK, D_KV), keys(D_KV, TK)],
        out_specs=rows(D_ATTN),
        out_shape=jax.ShapeDtypeStruct((b, s, D_ATTN), BF16),
        scratch_shapes=[pltpu.VMEM((nc, TK, TQ), F32),
                        pltpu.VMEM((nc, TK, TQ), F32),
                        pltpu.VMEM((IDX_DIM, IDX_HEADS * TQ), BF16),
                        pltpu.VMEM((D_KV, N_HEADS * TQ), BF16),
                        pltpu.VMEM((TK, N_HEADS * TQ), F32),
                        pltpu.VMEM((TK, N_HEADS * TQ), F32),
                        pltpu.VMEM((TK, N_HEADS * TQ), BF16),
                        pltpu.VMEM((TK, N_HEADS * TQ), BF16),
                        pltpu.VMEM((N_HEADS, SUBLANES, TQ), F32),
                        pltpu.VMEM((N_HEADS, SUBLANES, TQ), F32),
                        pltpu.VMEM((N_HEADS, SUBLANES, TQ), F32),
                        pltpu.VMEM((N_HEADS, SUBLANES, TQ), F32),
                        pltpu.VMEM((N_HEADS, SUBLANES, TQ), F32),
                        pltpu.VMEM((D_ATTN, TQ), F32),
                        pltpu.VMEM((SUBLANES, TQ), F32),
                        pltpu.VMEM((SUBLANES, TQ), F32),
                        pltpu.VMEM((SUBLANES, TQ), F32)],
        compiler_params=pltpu.CompilerParams(
            dimension_semantics=("parallel", "parallel"), vmem_limit_bytes=VMEM_LIMIT),
        name="dsa_attention",
    )(qi, wi, ki, q, k, vt)


def _tail_kernel(x_ref, ya_ref, yb_ref, wg_ref, wbr_ref, wo_ref, wup_ref, wdn_ref,
                 g1_ref, b1_ref, g2_ref, b2_ref, out_ref, *, alpha):
    part = TM_TAIL // TAIL_PARTS
    parts = [slice(p * part, (p + 1) * part) for p in range(TAIL_PARTS)]

    def mix_pre_ln(rows):
        x = x_ref[rows, :]
        xb = x.astype(BF16)
        merged = jnp.zeros((part, D_MODEL), F32)
        for n, y_ref in enumerate((ya_ref, yb_ref)):
            gate = _dot_nt(xb, wg_ref[n * D_MODEL:(n + 1) * D_MODEL, :])
            branch = jnp.dot(y_ref[rows, :], wbr_ref[n], preferred_element_type=F32)
            merged = merged + jax.nn.sigmoid(gate) * branch
        return alpha * x + jnp.dot(merged.astype(BF16), wo_ref[...], preferred_element_type=F32)

    def mlp_pre_ln(h):
        hb = h.astype(BF16)
        ff = jnp.zeros((part, D_MODEL), F32)
        for f in range(D_FF // FF_CHUNK):
            up = jnp.dot(hb, wup_ref[:, f * FF_CHUNK:(f + 1) * FF_CHUNK], preferred_element_type=F32)
            act = jnp.square(jnp.maximum(up, 0.0)).astype(BF16)
            ff = ff + jnp.dot(act, wdn_ref[f * FF_CHUNK:(f + 1) * FF_CHUNK, :], preferred_element_type=F32)
        return alpha * h + ff

    ln1 = lambda v: _layer_norm_rows(v, g1_ref[...], b1_ref[...])
    ln2 = lambda v: _layer_norm_rows(v, g2_ref[...], b2_ref[...])
    hs = [ln1(mix_pre_ln(rows)) for rows in parts]
    for rows, h in zip(parts, hs):
        out_ref[rows, :] = ln2(mlp_pre_ln(h))


def _run_tail(x2, ya2, yb2, wg, wbr, wo, wup, wdn, g1, b1, g2, b2, alpha):
    n, d = x2.shape
    const = lambda *shape: pl.BlockSpec(shape, lambda r: (0,) * len(shape),
                                        pipeline_mode=pl.Buffered(1))
    rows = lambda width: pl.BlockSpec((TM_TAIL, width), lambda r: (r, 0))
    return pl.pallas_call(
        functools.partial(_tail_kernel, alpha=alpha),
        grid=(n // TM_TAIL,),
        in_specs=[rows(d), rows(D_CONV), rows(D_ATTN),
                  const(N_BRANCH * d, d), const(N_BRANCH, D_CONV, d), const(d, d),
                  const(d, D_FF), const(D_FF, d),
                  const(1, d), const(1, d), const(1, d), const(1, d)],
        out_specs=rows(d),
        out_shape=jax.ShapeDtypeStruct((n, d), F32),
        compiler_params=pltpu.CompilerParams(
            dimension_semantics=("parallel",), vmem_limit_bytes=VMEM_LIMIT),
        name="merge_mlp",
    )(x2, ya2, yb2, wg, wbr, wo, wup, wdn, g1, b1, g2, b2)


def kernel(x, w_in, conv_w, idx_k_norm_g, idx_k_norm_b, w_branch, w_o, ln1_g, ln1_b, w_up, w_down, ln2_g, ln2_b):
    b, s, d = x.shape
    depth = w_in.shape[0]
    assert d == D_MODEL and s % TM_PROJ == 0 and (b * s) % TM_TAIL == 0 and TQ == TK
    k_sel = min(TOPK_MAX, s // 4)
    alpha = (2 * depth) ** 0.25
    for l in range(depth):
        wt = jnp.swapaxes(w_in[l], 0, 1)
        pad = lambda a: jnp.pad(a.astype(F32), (0, LANES - IDX_DIM)).reshape(1, LANES)
        later_weights = (w_branch[l].reshape(N_BRANCH * D_CONV, d), w_o[l], w_up[l], w_down[l])
        ya, q, k, vt, qi, ki, wi, wg, wbr, wo, wup, wdn = _run_proj(
            x, wt, conv_w[l], pad(idx_k_norm_g[l]), pad(idx_k_norm_b[l]), later_weights)
        yb = _run_attn(qi, wi, ki, q, k, vt, k_sel)
        row = lambda a: a.astype(F32).reshape(1, d)
        out = _run_tail(x.reshape(b * s, d), ya.reshape(b * s, D_CONV), yb.reshape(b * s, D_ATTN),
                        wg, wbr.reshape(N_BRANCH, D_CONV, d), wo, wup, wdn,
                        row(ln1_g[l]), row(ln1_b[l]), row(ln2_g[l]), row(ln2_b[l]), alpha)
        x = out.reshape(b, s, d)
    return x
```
